```python
import jax, jax.numpy as jnp
from jax import lax
import numpy as np


D_MODEL = 1024
BATCH = 16
SEQ = 4096
DEPTH = 1
DEC_BATCH = 128
DEC_SEQ = 1
PAST_LEN = 8192
PAGE_SIZE = 128

CONV_CH = D_MODEL // 2
CONV_K = 3
N_HEADS = 8
HEAD_DIM = D_MODEL // 16
ATTN_WIDTH = N_HEADS * HEAD_DIM
MIX_WIDTH = CONV_CH + ATTN_WIDTH
IN_WIDTH = 3 * CONV_CH + 3 * ATTN_WIDTH
D_FF = 2816
ROPE_THETA = 10000.0
EPS = 1e-6
PATTERNS = ((128, 1), (512, 4), (2048, 16))
WIN_MAX = max(w for w, _ in PATTERNS)

kernel_name = 'hymba_conv_dilated_swa_macaron_step'


def rmsnorm(x, g):
    xf = x.astype(jnp.float32)
    y = xf * lax.rsqrt(jnp.mean(xf * xf, axis=-1, keepdims=True) + EPS)
    return (y * g.astype(jnp.float32)).astype(x.dtype)


def half_ffn(x, g, w1, w3, w2):
    h = rmsnorm(x, g)
    return x + 0.5 * ((jax.nn.silu(h @ w1) * (h @ w3)) @ w2)


def rope(x, pos):
    dh = x.shape[-1]
    half = dh // 2
    inv = ROPE_THETA ** (-jnp.arange(half, dtype=jnp.float32) * 2.0 / dh)
    ang = pos[:, None] * inv[None, :]
    cos = jnp.cos(ang)[None, :, None, :]
    sin = jnp.sin(ang)[None, :, None, :]
    xf = x.astype(jnp.float32)
    x1, x2 = xf[..., :half], xf[..., half:]
    return jnp.concatenate([x1 * cos - x2 * sin, x2 * cos + x1 * sin], axis=-1).astype(x.dtype)


def split_mix(z):
    W, A = CONV_CH, ATTN_WIDTH
    b_g = z[..., :W]
    c_g = z[..., W:2 * W]
    x_in = z[..., 2 * W:3 * W]
    o = 3 * W
    heads = z.shape[:-1] + (N_HEADS, HEAD_DIM)
    q = z[..., o:o + A].reshape(heads)
    k = z[..., o + A:o + 2 * A].reshape(heads)
    v = z[..., o + 2 * A:o + 3 * A].reshape(heads)
    return b_g, c_g, x_in, q, k, v


def causal_conv(u_ext, w):
    s = u_ext.shape[1] - (CONV_K - 1)
    return sum(w[j] * u_ext[:, j:j + s] for j in range(CONV_K))


def dilated_band_prompt(q, k, v, window, dilation):
    b, s, h, dh = q.shape
    nb = window // dilation
    seg = nb * dilation
    s_pad = -(-s // seg) * seg
    n_blk = s_pad // seg

    def to_blocks(t):
        t = jnp.pad(t, ((0, 0), (0, s_pad - s), (0, 0), (0, 0)))
        t = t.reshape(b, s_pad // dilation, dilation, h, dh).transpose(0, 2, 1, 3, 4)
        return t.reshape(b, dilation, n_blk, nb, h, dh)

    def with_prev(t):
        prev = jnp.pad(t[:, :, :-1], ((0, 0), (0, 0), (1, 0), (0, 0), (0, 0), (0, 0)))
        return jnp.concatenate([prev, t], axis=3)

    qb = to_blocks(q)
    kb = with_prev(to_blocks(k))
    vb = with_prev(to_blocks(v))
    scores = jnp.einsum('brnqhd,brnkhd->brnhqk', qb, kb,
                        preferred_element_type=jnp.float32) * (dh ** -0.5)
    qi = jnp.arange(nb)[:, None]
    kj = jnp.arange(2 * nb)[None, :]
    dist = qi + nb - kj
    band = (dist >= 0) & (dist <= nb)
    first = jnp.arange(n_blk)[:, None, None] == 0
    valid = band[None] & ~(first & (kj[None] < nb))
    scores = jnp.where(valid[None, None, :, None], scores, -jnp.inf)
    lse = jax.nn.logsumexp(scores, axis=-1)
    p = jnp.exp(scores - lse[..., None])
    o = jnp.einsum('brnhqk,brnkhd->brnqhd', p, vb.astype(jnp.float32))
    o = o.reshape(b, dilation, s_pad // dilation, h, dh).transpose(0, 2, 1, 3, 4)
    o = o.reshape(b, s_pad, h, dh)[:, :s]
    lse = lse.transpose(0, 1, 2, 4, 3).reshape(b, dilation, s_pad // dilation, h)
    lse = lse.transpose(0, 2, 1, 3).reshape(b, s_pad, h)[:, :s]
    return o, lse


def dilated_gather_sample(q, k_ext, v_ext, window, dilation):
    b, ds, h, dh = q.shape
    L = k_ext.shape[1]
    wbuf = L - ds
    n_off = window // dilation + 1
    idx = wbuf + jnp.arange(ds)[:, None] - jnp.arange(n_off)[None, :] * dilation
    valid = idx >= 0
    idx = jnp.clip(idx, 0, L - 1)
    kg = k_ext[:, idx]
    vg = v_ext[:, idx]
    scores = jnp.einsum('bqhd,bqjhd->bqhj', q, kg,
                        preferred_element_type=jnp.float32) * (dh ** -0.5)
    scores = jnp.where(valid[None, :, None, :], scores, -jnp.inf)
    lse = jax.nn.logsumexp(scores, axis=-1)
    p = jnp.exp(scores - lse[..., None])
    o = jnp.einsum('bqhj,bqjhd->bqhd', p, vg.astype(jnp.float32))
    return o, lse


def combine_patterns(outs, lses):
    w = jax.nn.softmax(jnp.stack(lses, axis=0), axis=0)
    return jnp.sum(w[..., None] * jnp.stack(outs, axis=0), axis=0)


def mixer_prompt(h, w_in, conv_w):
    b_g, c_g, x_in, q, k, v = split_mix(h @ w_in)
    pos = jnp.arange(h.shape[1], dtype=jnp.float32)
    q, k = rope(q, pos), rope(k, pos)
    u = c_g * x_in
    conv_out = causal_conv(jnp.pad(u, ((0, 0), (CONV_K - 1, 0), (0, 0))), conv_w)
    res = [dilated_band_prompt(q, k, v, w, d) for (w, d) in PATTERNS]
    attn = combine_patterns([r[0] for r in res], [r[1] for r in res])
    keep = min(WIN_MAX, h.shape[1])
    state = (u[:, -(CONV_K - 1):], k[:, -keep:], v[:, -keep:])
    return b_g, conv_out, attn.astype(h.dtype), state


def mixer_sample(h, conv_state, k_buf, v_buf, w_in, conv_w):
    b_g, c_g, x_in, q, k, v = split_mix(h @ w_in)
    pos = PAST_LEN + jnp.arange(h.shape[1], dtype=jnp.float32)
    q, k = rope(q, pos), rope(k, pos)
    u = c_g * x_in
    u_ext = jnp.concatenate([conv_state.astype(u.dtype), u], axis=1)
    conv_out = causal_conv(u_ext, conv_w)
    k_ext = jnp.concatenate([k_buf.astype(k.dtype), k], axis=1)
    v_ext = jnp.concatenate([v_buf.astype(v.dtype), v], axis=1)
    res = [dilated_gather_sample(q, k_ext, v_ext, w, d) for (w, d) in PATTERNS]
    attn = combine_patterns([r[0] for r in res], [r[1] for r in res])
    wbuf = k_buf.shape[1]
    state = (u_ext[:, -(CONV_K - 1):], k_ext[:, -wbuf:], v_ext[:, -wbuf:])
    return b_g, conv_out, attn.astype(h.dtype), state


def trunk_layer(x, ffn1, mixw, ffn2, past):
    g_mix, w_in, conv_w, g_conv, g_attn, w_out = mixw
    x = half_ffn(x, *ffn1)
    h = rmsnorm(x, g_mix)
    if past is None:
        b_g, conv_out, attn, state = mixer_prompt(h, w_in, conv_w)
    else:
        b_g, conv_out, attn, state = mixer_sample(h, past[0], past[1], past[2], w_in, conv_w)
    y_c = rmsnorm(b_g * conv_out, g_conv)
    y_a = rmsnorm(attn.reshape(attn.shape[:2] + (ATTN_WIDTH,)), g_attn)
    x = x + jnp.concatenate([y_c, y_a], axis=-1) @ w_out
    x = half_ffn(x, *ffn2)
    return x, state


def setup_inputs(seed: int = 0) -> dict:
    key = jax.random.key(seed)
    ks = jax.random.split(key, 24)
    f32 = jnp.float32
    wbuf = min(WIN_MAX, PAST_LEN)

    def nrm(k, shape, scale):
        return jax.random.normal(k, shape, f32) * scale

    def gain(k, shape):
        return 1.0 + 0.01 * jax.random.normal(k, shape, f32)

    return {
        'x_prompt': nrm(ks[0], (BATCH, SEQ, D_MODEL), 1.0),
        'x_sample': nrm(ks[1], (DEC_BATCH, DEC_SEQ, D_MODEL), 1.0),
        'state_conv': nrm(ks[2], (DEPTH, DEC_BATCH, CONV_K - 1, CONV_CH), 1.0),
        'cache_k': nrm(ks[3], (DEPTH, DEC_BATCH, wbuf, N_HEADS, HEAD_DIM), 1.0),
        'cache_v': nrm(ks[4], (DEPTH, DEC_BATCH, wbuf, N_HEADS, HEAD_DIM), 1.0),
        'norm_ffn1': gain(ks[5], (DEPTH, D_MODEL)),
        'ffn1_w1': nrm(ks[6], (DEPTH, D_MODEL, D_FF), D_MODEL ** -0.5),
        'ffn1_w3': nrm(ks[7], (DEPTH, D_MODEL, D_FF), D_MODEL ** -0.5),
        'ffn1_w2': nrm(ks[8], (DEPTH, D_FF, D_MODEL), D_FF ** -0.5),
        'norm_mix': gain(ks[9], (DEPTH, D_MODEL)),
        'w_in': nrm(ks[10], (DEPTH, D_MODEL, IN_WIDTH), D_MODEL ** -0.5),
        'conv_w': nrm(ks[11], (DEPTH, CONV_K, CONV_CH), CONV_K ** -0.5),
        'out_norm_conv': gain(ks[12], (DEPTH, CONV_CH)),
        'out_norm_attn': gain(ks[13], (DEPTH, ATTN_WIDTH)),
        'w_out': nrm(ks[14], (DEPTH, MIX_WIDTH, D_MODEL), MIX_WIDTH ** -0.5),
        'norm_ffn2': gain(ks[15], (DEPTH, D_MODEL)),
        'ffn2_w1': nrm(ks[16], (DEPTH, D_MODEL, D_FF), D_MODEL ** -0.5),
        'ffn2_w3': nrm(ks[17], (DEPTH, D_MODEL, D_FF), D_MODEL ** -0.5),
        'ffn2_w2': nrm(ks[18], (DEPTH, D_FF, D_MODEL), D_FF ** -0.5),
        'norm_final': gain(ks[19], (D_MODEL,)),
    }


def reference(x_prompt, x_sample, state_conv, cache_k, cache_v, norm_ffn1, ffn1_w1, ffn1_w3,
              ffn1_w2, norm_mix, w_in, conv_w, out_norm_conv, out_norm_attn, w_out, norm_ffn2,
              ffn2_w1, ffn2_w3, ffn2_w2, norm_final):
    yp, ys = x_prompt, x_sample
    st_p, st_s = [], []
    for l in range(DEPTH):
        ffn1 = (norm_ffn1[l], ffn1_w1[l], ffn1_w3[l], ffn1_w2[l])
        ffn2 = (norm_ffn2[l], ffn2_w1[l], ffn2_w3[l], ffn2_w2[l])
        mixw = (norm_mix[l], w_in[l], conv_w[l], out_norm_conv[l], out_norm_attn[l], w_out[l])
        yp, sp = trunk_layer(yp, ffn1, mixw, ffn2, None)
        ys, ss = trunk_layer(ys, ffn1, mixw, ffn2, (state_conv[l], cache_k[l], cache_v[l]))
        st_p.append(sp)
        st_s.append(ss)
    y_prompt = rmsnorm(yp, norm_final)
    y_sample = rmsnorm(ys, norm_final)
    state_conv_prompt = jnp.stack([s[0] for s in st_p], axis=0)
    state_conv_sample = jnp.stack([s[0] for s in st_s], axis=0)
    cache_k_prompt = jnp.stack([s[1] for s in st_p], axis=0)
    cache_v_prompt = jnp.stack([s[2] for s in st_p], axis=0)
    cache_k_sample = jnp.stack([s[1] for s in st_s], axis=0)
    cache_v_sample = jnp.stack([s[2] for s in st_s], axis=0)
    return (y_prompt, y_sample, state_conv_prompt, state_conv_sample, cache_k_prompt, cache_v_prompt, cache_k_sample, cache_v_sample)
```

```python
import functools

import jax
import jax.numpy as jnp
import numpy as np
from jax import lax
from jax.experimental import pallas as pl
from jax.experimental.pallas import tpu as pltpu

F32 = jnp.float32
BF16 = jnp.bfloat16

D_MODEL = 1024
CONV_CH = D_MODEL // 2
CONV_K = 3
N_HEADS = 8
HEAD_DIM = D_MODEL // 16
ATTN_WIDTH = N_HEADS * HEAD_DIM
D_FF = 2816
ROPE_THETA = 10000.0
EPS = 1e-6
PATTERNS = ((128, 1), (512, 4), (2048, 16))
WIN_MAX = max(w for w, _ in PATTERNS)
PAST_LEN = 8192

LANES = 128
HEADS_PER_TILE = LANES // HEAD_DIM
N_HEAD_TILES = ATTN_WIDTH // LANES
NB = 128
MAX_DIL = max(d for _, d in PATTERNS)
CHUNK_POS = NB * MAX_DIL
FF_CHUNK = 256
ROW_TILE = 512
VMEM_LIMIT = 48 * 1024 * 1024

assert all(w // d == NB for w, d in PATTERNS)
assert [d for _, d in PATTERNS] == [1, 4, 16]


def _rms(x, g):
    return x * lax.rsqrt(jnp.mean(x * x, axis=-1, keepdims=True) + EPS) * g


def _dot(a, b):
    return jnp.dot(a, b, preferred_element_type=F32)


def _swiglu(h, w1_ref, w3_ref, w2_ref):
    acc = None
    for c in range(D_FF // FF_CHUNK):
        sl = slice(c * FF_CHUNK, (c + 1) * FF_CHUNK)
        a = _dot(h, w1_ref[:, sl])
        b = _dot(h, w3_ref[:, sl])
        t = (a * jax.nn.sigmoid(a) * b).astype(BF16)
        p = _dot(t, w2_ref[sl, :])
        acc = p if acc is None else acc + p
    return acc


def _const_spec(shape):
    return pl.BlockSpec(shape, lambda *_: (0,) * len(shape))


def _params(n_grid):
    return pltpu.CompilerParams(dimension_semantics=("arbitrary",) * n_grid,
                                vmem_limit_bytes=VMEM_LIMIT)


def _ffn_kernel(x_ref, g_ref, w1_ref, w3_ref, w2_ref, o_ref):
    x = x_ref[...]
    h = _rms(x, g_ref[...]).astype(BF16)
    o_ref[...] = x + 0.5 * _swiglu(h, w1_ref, w3_ref, w2_ref)


def _ffn(x, g, w1, w3, w2):
    m = x.shape[0]
    tm = min(ROW_TILE, m)
    return pl.pallas_call(
        _ffn_kernel,
        out_shape=jax.ShapeDtypeStruct(x.shape, F32),
        grid=(m // tm,),
        in_specs=[pl.BlockSpec((tm, D_MODEL), lambda i: (i, 0)),
                  _const_spec((1, D_MODEL)),
                  _const_spec((D_MODEL, D_FF)), _const_spec((D_MODEL, D_FF)),
                  _const_spec((D_FF, D_MODEL))],
        out_specs=pl.BlockSpec((tm, D_MODEL), lambda i: (i, 0)),
        compiler_params=_params(1),
        name="ffn1",
    )(x, g, w1, w3, w2)


def _rope_tables(pos):
    half = HEAD_DIM // 2
    inv = ROPE_THETA ** (-jnp.arange(half, dtype=F32) * 2.0 / HEAD_DIM)
    ang = pos[:, None] * inv[None, :]
    cos, sin = jnp.cos(ang), jnp.sin(ang)
    cos_t = jnp.concatenate([cos, cos] * HEADS_PER_TILE, axis=-1)
    sin_t = jnp.concatenate([-sin, sin] * HEADS_PER_TILE, axis=-1)
    return cos_t, sin_t


def _rope_tile(x, cos_t, sin_t):
    half = HEAD_DIM // 2
    lane = lax.broadcasted_iota(jnp.int32, x.shape, 1)
    first_half = (lane % HEAD_DIM) < half
    partner = jnp.where(first_half,
                        pltpu.roll(x, LANES - half, axis=1),
                        pltpu.roll(x, half, axis=1))
    return x * cos_t + partner * sin_t


def _mix_in_prompt_kernel(x_ref, g_ref, w_ref, cw_ref, gc_ref, cos_ref, sin_ref,
                          yc_ref, q_ref, k_ref, v_ref, kl_ref, vl_ref, st_ref, carry_ref,
                          *, n_keep_tiles):
    i = pl.program_id(1)
    n_i = pl.num_programs(1)
    tm = x_ref.shape[1]
    w, a = CONV_CH, ATTN_WIDTH
    h = _rms(x_ref[0], g_ref[...]).astype(BF16)

    b_g = _dot(h, w_ref[:, 0:w])
    u = _dot(h, w_ref[:, w:2 * w]) * _dot(h, w_ref[:, 2 * w:3 * w])

    @pl.when(i == 0)
    def _():
        carry_ref[...] = jnp.zeros_like(carry_ref)

    prev2 = carry_ref[0:1, :]
    prev1 = carry_ref[1:2, :]
    row = lax.broadcasted_iota(jnp.int32, (tm, w), 0)
    u1 = jnp.where(row == 0, prev1, pltpu.roll(u, 1, axis=0))
    u2 = jnp.where(row == 0, prev2, jnp.where(row == 1, prev1, pltpu.roll(u, 2, axis=0)))
    conv = cw_ref[0:1, :] * u2 + cw_ref[1:2, :] * u1 + cw_ref[2:3, :] * u
    yc_ref[0] = _rms(b_g * conv, gc_ref[...])
    carry_ref[0:2, :] = u[tm - 2:tm, :]
    st_ref[0] = u[tm - 2:tm, :]

    o = 3 * w
    q = _dot(h, w_ref[:, o:o + a])
    k = _dot(h, w_ref[:, o + a:o + 2 * a])
    v = _dot(h, w_ref[:, o + 2 * a:o + 3 * a])
    cos_t, sin_t = cos_ref[...], sin_ref[...]
    k_tiles = []
    for t in range(N_HEAD_TILES):
        sl = slice(t * LANES, (t + 1) * LANES)
        q_ref[0, t] = _rope_tile(q[:, sl], cos_t, sin_t) * (HEAD_DIM ** -0.5)
        kt = _rope_tile(k[:, sl], cos_t, sin_t)
        k_ref[0, t] = kt
        v_ref[0, t] = v[:, sl]
        k_tiles.append(kt)

    @pl.when(i >= n_i - n_keep_tiles)
    def _():
        kl_ref[0] = jnp.concatenate(k_tiles, axis=1)
        vl_ref[0] = v


def _mix_in_prompt(x, g, w_in, conv_w, g_conv, cos_t, sin_t):
    b, s, _ = x.shape
    tm = min(ROW_TILE, s)
    n_i = s // tm
    keep = min(WIN_MAX, s)
    n_keep = keep // tm
    tile_map = lambda bi, i: (bi, 0, i, 0)
    keep_map = lambda bi, i: (bi, jnp.maximum(i - (n_i - n_keep), 0), 0)
    head_major = jax.ShapeDtypeStruct((b, N_HEAD_TILES, s, LANES), F32)
    return pl.pallas_call(
        functools.partial(_mix_in_prompt_kernel, n_keep_tiles=n_keep),
        out_shape=(jax.ShapeDtypeStruct((b, s, CONV_CH), F32),
                   head_major, head_major, head_major,
                   jax.ShapeDtypeStruct((b, keep, ATTN_WIDTH), F32),
                   jax.ShapeDtypeStruct((b, keep, ATTN_WIDTH), F32),
                   jax.ShapeDtypeStruct((b, CONV_K - 1, CONV_CH), F32)),
        grid=(b, n_i),
        in_specs=[pl.BlockSpec((1, tm, D_MODEL), lambda bi, i: (bi, i, 0)),
                  _const_spec((1, D_MODEL)),
                  _const_spec(w_in.shape),
                  _const_spec((CONV_K, CONV_CH)),
                  _const_spec((1, CONV_CH)),
                  pl.BlockSpec((tm, LANES), lambda bi, i: (i, 0)),
                  pl.BlockSpec((tm, LANES), lambda bi, i: (i, 0))],
        out_specs=(pl.BlockSpec((1, tm, CONV_CH), lambda bi, i: (bi, i, 0)),
                   pl.BlockSpec((1, N_HEAD_TILES, tm, LANES), tile_map),
                   pl.BlockSpec((1, N_HEAD_TILES, tm, LANES), tile_map),
                   pl.BlockSpec((1, N_HEAD_TILES, tm, LANES), tile_map),
                   pl.BlockSpec((1, tm, ATTN_WIDTH), keep_map),
                   pl.BlockSpec((1, tm, ATTN_WIDTH), keep_map),
                   pl.BlockSpec((1, CONV_K - 1, CONV_CH), lambda bi, i: (bi, 0, 0))),
        scratch_shapes=[pltpu.VMEM((8, CONV_CH), F32)],
        compiler_params=_params(2),
        name="mix_in_prompt",
    )(x, g, w_in, conv_w, g_conv, cos_t, sin_t)


def _band_bias():
    out = np.zeros((len(PATTERNS), 2, NB, 2 * NB), np.float32)
    for p, (_, d) in enumerate(PATTERNS):
        g = MAX_DIL // d
        qr, kr = NB // g, 2 * NB // g
        qi = np.arange(NB)
        kj = np.arange(2 * NB)
        qm, qa = qi // qr, qi % qr
        km, ka = kj // kr, kj % kr
        for variant, off in enumerate((NB // g, 0)):
            dist = g * (qa[:, None] - ka[None, :] + off) + (qm[:, None] - km[None, :])
            out[p, variant] = np.where((dist >= 0) & (dist <= NB), 0.0, -np.inf)
    return out.reshape(len(PATTERNS) * 2, NB, 2 * NB)


def _attn_block(q, k, v, bias, old):
    lo = lax.broadcasted_iota(jnp.int32, (NB, LANES), 1) < HEAD_DIM
    lo_kv = lax.broadcasted_iota(jnp.int32, (2 * NB, LANES), 1) < HEAD_DIM
    kb = k.astype(BF16)
    nt = (((1,), (1,)), ((), ()))
    heads = ((jnp.where(lo, q, 0.0), jnp.where(lo_kv, v, 0.0)),
             (jnp.where(lo, 0.0, q), jnp.where(lo_kv, 0.0, v)))
    m_new, row_sum, pv = [], [], None
    for hi, (qh, vh) in enumerate(heads):
        s = lax.dot_general(qh.astype(BF16), kb, nt, preferred_element_type=F32) + bias
        row_max = jnp.max(s, axis=-1, keepdims=True)
        if old is None:
            mh = jnp.broadcast_to(row_max, (NB, LANES))
        else:
            mh = jnp.maximum(old[1 + hi], row_max)
        p = jnp.exp(s - jnp.concatenate([mh, mh], axis=1))
        row_sum.append(jnp.sum(p, axis=-1, keepdims=True))
        contrib = _dot(p.astype(BF16), vh.astype(BF16))
        pv = contrib if pv is None else pv + contrib
        m_new.append(mh)
    rs = jnp.where(lo, row_sum[0], row_sum[1])
    if old is None:
        return pv, m_new[0], m_new[1], rs
    alpha = jnp.where(lo, jnp.exp(old[1] - m_new[0]), jnp.exp(old[2] - m_new[1]))
    return alpha * old[0] + pv, m_new[0], m_new[1], alpha * old[3] + rs


def _attn_prompt_kernel(bias_ref, q_ref, k_ref, v_ref, o_ref, acc_ref, m0_ref, m1_ref, l_ref):
    c = pl.program_id(2)
    row0 = c * NB
    state_refs = (acc_ref, m0_ref, m1_ref, l_ref)

    def tile(r):
        return slice(r * LANES, (r + 1) * LANES)

    def gather(ref, start, rows, classes):
        return jnp.concatenate([ref[0, 0, pl.ds(start, rows), tile(r)] for r in classes], axis=0)

    def window(g, a0):
        start = row0 + a0 - NB // g
        first = start < 0
        return jnp.maximum(start, 0), first.astype(jnp.int32)

    p16 = 2
    for r in range(MAX_DIL):
        ks, variant = window(1, 0)
        ks = pl.multiple_of(ks, NB)
        new = _attn_block(q_ref[0, 0, :, tile(r)],
                          k_ref[0, 0, pl.ds(ks, 2 * NB), tile(r)],
                          v_ref[0, 0, pl.ds(ks, 2 * NB), tile(r)],
                          bias_ref[2 * p16 + variant], None)
        for ref, val in zip(state_refs, new):
            ref[:, tile(r)] = val

    p4, g4 = 1, 4
    qr4 = NB // g4
    for r4 in range(4):
        classes = [4 * m + r4 for m in range(g4)]

        def body4(t, carry, classes=classes):
            a0 = pl.multiple_of(t * qr4, qr4)
            ks, variant = window(g4, a0)
            ks = pl.multiple_of(ks, qr4)
            old = tuple(jnp.concatenate([ref[pl.ds(a0, qr4), tile(r)] for r in classes], axis=0)
                        for ref in state_refs)
            new = _attn_block(gather(q_ref, a0, qr4, classes),
                              gather(k_ref, ks, 2 * qr4, classes),
                              gather(v_ref, ks, 2 * qr4, classes),
                              bias_ref[2 * p4 + variant], old)
            for ref, val in zip(state_refs, new):
                for m, r in enumerate(classes):
                    ref[pl.ds(a0, qr4), tile(r)] = val[m * qr4:(m + 1) * qr4]
            return carry

        lax.fori_loop(0, NB // qr4, body4, 0)

    p1, g1 = 0, 16
    qr1 = NB // g1
    classes1 = list(range(MAX_DIL))

    def body1(t, carry):
        a0 = pl.multiple_of(t * qr1, qr1)
        ks, variant = window(g1, a0)
        ks = pl.multiple_of(ks, qr1)
        old = tuple(jnp.concatenate([ref[pl.ds(a0, qr1), tile(r)] for r in classes1], axis=0)
                    for ref in state_refs)
        acc, _, _, l = _attn_block(gather(q_ref, a0, qr1, classes1),
                                   gather(k_ref, ks, 2 * qr1, classes1),
                                   gather(v_ref, ks, 2 * qr1, classes1),
                                   bias_ref[2 * p1 + variant], old)
        out = acc / l
        for r in classes1:
            o_ref[0, 0, pl.ds(a0, qr1), tile(r)] = out[r * qr1:(r + 1) * qr1]
        return carry

    lax.fori_loop(0, NB // qr1, body1, 0)


def _attn_prompt(q, k, v):
    b, _, s, _ = q.shape
    assert s % CHUNK_POS == 0
    rows = s // MAX_DIL
    width = MAX_DIL * LANES
    view = lambda t: t.reshape(b, N_HEAD_TILES, rows, width)
    bias = jnp.asarray(_band_bias())
    out = pl.pallas_call(
        _attn_prompt_kernel,
        out_shape=jax.ShapeDtypeStruct((b, N_HEAD_TILES, rows, width), F32),
        grid=(b, N_HEAD_TILES, s // CHUNK_POS),
        in_specs=[_const_spec(bias.shape),
                  pl.BlockSpec((1, 1, NB, width), lambda bi, t, c: (bi, t, c, 0)),
                  pl.BlockSpec((1, 1, rows, width), lambda bi, t, c: (bi, t, 0, 0)),
                  pl.BlockSpec((1, 1, rows, width), lambda bi, t, c: (bi, t, 0, 0))],
        out_specs=pl.BlockSpec((1, 1, NB, width), lambda bi, t, c: (bi, t, c, 0)),
        scratch_shapes=[pltpu.VMEM((NB, width), F32)] * 4,
        compiler_params=_params(3),
        name="attn_prompt",
    )(bias, view(q), view(k), view(v))
    return out.reshape(b, N_HEAD_TILES, s, LANES)


def _mix_out_kernel(x_ref, yc_ref, at_ref, ga_ref, wo_ref, g_ref, w1_ref, w3_ref, w2_ref,
                    gf_ref, o_ref):
    attn = jnp.concatenate([at_ref[0, t] for t in range(N_HEAD_TILES)], axis=1)
    y_a = _rms(attn, ga_ref[...])
    mixed = jnp.concatenate([yc_ref[0], y_a], axis=1).astype(BF16)
    x = x_ref[0] + _dot(mixed, wo_ref[...])
    h = _rms(x, g_ref[...]).astype(BF16)
    x = x + 0.5 * _swiglu(h, w1_ref, w3_ref, w2_ref)
    o_ref[0] = _rms(x, gf_ref[...])


def _mix_out(x, yc, attn, g_attn, w_out, g, w1, w3, w2, g_final):
    b, s, _ = x.shape
    tm = min(ROW_TILE, s)
    row_map = lambda bi, i: (bi, i, 0)
    return pl.pallas_call(
        _mix_out_kernel,
        out_shape=jax.ShapeDtypeStruct(x.shape, F32),
        grid=(b, s // tm),
        in_specs=[pl.BlockSpec((1, tm, D_MODEL), row_map),
                  pl.BlockSpec((1, tm, CONV_CH), row_map),
                  pl.BlockSpec((1, N_HEAD_TILES, tm, LANES), lambda bi, i: (bi, 0, i, 0)),
                  _const_spec((1, ATTN_WIDTH)),
                  _const_spec(w_out.shape),
                  _const_spec((1, D_MODEL)),
                  _const_spec((D_MODEL, D_FF)), _const_spec((D_MODEL, D_FF)),
                  _const_spec((D_FF, D_MODEL)),
                  _const_spec((1, D_MODEL))],
        out_specs=pl.BlockSpec((1, tm, D_MODEL), row_map),
        compiler_params=_params(2),
        name="mix_out_ffn2",
    )(x, yc, attn, g_attn, w_out, g, w1, w3, w2, g_final)


def _mix_in_sample_kernel(x_ref, g_ref, w_ref, cw_ref, gc_ref, cos_ref, sin_ref, s0_ref, s1_ref,
                          yc_ref, q_ref, k_ref, v_ref, u_ref):
    w, a = CONV_CH, ATTN_WIDTH
    h = _rms(x_ref[...], g_ref[...]).astype(BF16)
    b_g = _dot(h, w_ref[:, 0:w])
    u = _dot(h, w_ref[:, w:2 * w]) * _dot(h, w_ref[:, 2 * w:3 * w])
    conv = cw_ref[0:1, :] * s0_ref[...] + cw_ref[1:2, :] * s1_ref[...] + cw_ref[2:3, :] * u
    yc_ref[...] = _rms(b_g * conv, gc_ref[...])
    u_ref[...] = u
    o = 3 * w
    q = _dot(h, w_ref[:, o:o + a])
    k = _dot(h, w_ref[:, o + a:o + 2 * a])
    v_ref[...] = _dot(h, w_ref[:, o + 2 * a:o + 3 * a])
    cos_t, sin_t = cos_ref[...], sin_ref[...]
    for t in range(N_HEAD_TILES):
        sl = slice(t * LANES, (t + 1) * LANES)
        q_ref[:, sl] = _rope_tile(q[:, sl], cos_t, sin_t) * (HEAD_DIM ** -0.5)
        k_ref[:, sl] = _rope_tile(k[:, sl], cos_t, sin_t)


def _mix_in_sample(x, g, w_in, conv_w, g_conv, cos_t, sin_t, st0, st1):
    n = x.shape[0]
    vec = lambda width: jax.ShapeDtypeStruct((n, width), F32)
    return pl.pallas_call(
        _mix_in_sample_kernel,
        out_shape=(vec(CONV_CH), vec(ATTN_WIDTH), vec(ATTN_WIDTH), vec(ATTN_WIDTH), vec(CONV_CH)),
        compiler_params=pltpu.CompilerParams(vmem_limit_bytes=VMEM_LIMIT),
        name="mix_in_sample",
    )(x, g, w_in, conv_w, g_conv, cos_t, sin_t, st0, st1)


def _attn_sample_kernel(q_ref, kn_ref, vn_ref, ck1, ck4, ck16, cv1, cv4, cv16, o_ref,
                        kbuf, vbuf, sem):
    b = pl.program_id(0)
    n_b = pl.num_programs(0)
    n_pat = len(PATTERNS)
    views = ((ck1, ck4, ck16), (cv1, cv4, cv16))
    bufs = (kbuf, vbuf)

    def copies(bb, slot):
        out = []
        for kv in range(2):
            for p, (_, d) in enumerate(PATTERNS):
                src = views[kv][p]
                rows = pl.ds(src.shape[1] - NB, NB)
                src = src.at[bb, rows] if d == 1 else src.at[bb, rows, 0]
                out.append(pltpu.make_async_copy(src, bufs[kv].at[slot, p], sem.at[slot, kv, p]))
        return out

    slot = b % 2

    @pl.when(b == 0)
    def _():
        for cp in copies(0, 0):
            cp.start()

    @pl.when(b + 1 < n_b)
    def _():
        for cp in copies(b + 1, 1 - slot):
            cp.start()

    for cp in copies(b, slot):
        cp.wait()

    q = q_ref[0]
    k_new, v_new = kn_ref[0], vn_ref[0]
    s_self = jnp.sum(q * k_new, axis=-1, keepdims=True)
    scores = [jnp.sum(kbuf[slot, p] * q[None], axis=-1, keepdims=True) for p in range(n_pat)]
    m = s_self
    for s in scores:
        m = jnp.maximum(m, jnp.max(s, axis=0))
    e_self = float(n_pat) * jnp.exp(s_self - m)
    den = e_self
    num = e_self * v_new
    for p, s in enumerate(scores):
        e = jnp.exp(s - m[None])
        den = den + jnp.sum(e, axis=0)
        num = num + jnp.sum(e * vbuf[slot, p], axis=0)
    o_ref[0] = num / den


def _attn_sample(q, k_new, v_new, cache_k, cache_v):
    n, w = cache_k.shape[:2]
    assert w == WIN_MAX, "every window must lie inside the cached range"
    views = []
    for cache in (cache_k, cache_v):
        for _, d in PATTERNS:
            views.append(cache if d == 1 else cache.reshape(n, w // d, d, N_HEADS, HEAD_DIM))
    vec_spec = pl.BlockSpec((1, N_HEADS, HEAD_DIM), lambda b: (b, 0, 0))
    buf = pltpu.VMEM((2, len(PATTERNS), NB, N_HEADS, HEAD_DIM), F32)
    return pl.pallas_call(
        _attn_sample_kernel,
        out_shape=jax.ShapeDtypeStruct((n, N_HEADS, HEAD_DIM), F32),
        grid=(n,),
        in_specs=[vec_spec, vec_spec, vec_spec] + [pl.BlockSpec(memory_space=pl.ANY)] * 6,
        out_specs=vec_spec,
        scratch_shapes=[buf, buf, pltpu.SemaphoreType.DMA((2, 2, len(PATTERNS)))],
        compiler_params=_params(1),
        name="attn_sample",
    )(q, k_new, v_new, *views)


CACHE_COPY_GROUPS = 8


def _cache_shift_kernel(ck, cv, kn, vn, ok, ov, sem):
    n, w = ck.shape[:2]
    per = n // CACHE_COPY_GROUPS
    cps = []
    for i, (src, new, dst) in enumerate(((ck, kn, ok), (cv, vn, ov))):
        for gi in range(CACHE_COPY_GROUPS):
            sl = pl.ds(gi * per, per)
            cps.append(pltpu.make_async_copy(src.at[sl, pl.ds(1, w - 1)],
                                             dst.at[sl, pl.ds(0, w - 1)], sem.at[i, gi]))
        cps.append(pltpu.make_async_copy(new, dst.at[:, w - 1], sem.at[i, CACHE_COPY_GROUPS]))
    for cp in cps:
        cp.start()
    for cp in cps:
        cp.wait()


def _cache_shift(cache_k, cache_v, k_new, v_new):
    n = cache_k.shape[0]
    assert n % CACHE_COPY_GROUPS == 0
    any_spec = pl.BlockSpec(memory_space=pl.ANY)
    vmem_spec = pl.BlockSpec(memory_space=pltpu.VMEM)
    shape = jax.ShapeDtypeStruct(cache_k.shape, cache_k.dtype)
    return pl.pallas_call(
        _cache_shift_kernel,
        out_shape=(shape, shape),
        in_specs=[any_spec, any_spec, vmem_spec, vmem_spec],
        out_specs=(any_spec, any_spec),
        scratch_shapes=[pltpu.SemaphoreType.DMA((2, CACHE_COPY_GROUPS + 1))],
        name="cache_shift",
    )(cache_k, cache_v, k_new, v_new)


def _layer_prompt(x, wts):
    b, s, _ = x.shape
    cos_t, sin_t = _rope_tables(jnp.arange(s, dtype=F32))
    x1 = _ffn(x.reshape(b * s, D_MODEL), *wts["ffn1"]).reshape(b, s, D_MODEL)
    yc, q, k, v, k_last, v_last, u_state = _mix_in_prompt(
        x1, wts["g_mix"], wts["w_in"], wts["conv_w"], wts["g_conv"], cos_t, sin_t)
    attn = _attn_prompt(q, k, v)
    y = _mix_out(x1, yc, attn, wts["g_attn"], wts["w_out"], *wts["ffn2"], wts["g_final"])
    keep = k_last.shape[1]
    heads = (b, keep, N_HEADS, HEAD_DIM)
    return y, (u_state, k_last.reshape(heads), v_last.reshape(heads))


def _layer_sample(x, conv_state, cache_k, cache_v, wts):
    n = x.shape[0]
    assert x.shape[1] == 1, "one new token per cached sequence"
    cos_t, sin_t = _rope_tables(jnp.full((1,), PAST_LEN, dtype=F32))
    x1 = _ffn(x.reshape(n, D_MODEL), *wts["ffn1"])
    st0, st1 = conv_state[:, 0], conv_state[:, 1]
    yc, q, k, v, u = _mix_in_sample(x1, wts["g_mix"], wts["w_in"], wts["conv_w"], wts["g_conv"],
                                    cos_t, sin_t, st0, st1)
    heads = (n, N_HEADS, HEAD_DIM)
    k_new, v_new = k.reshape(heads), v.reshape(heads)
    attn = _attn_sample(q.reshape(heads), k_new, v_new, cache_k, cache_v)
    attn = attn.reshape(n, N_HEAD_TILES, LANES).transpose(1, 0, 2)[None]
    y = _mix_out(x1[None], yc[None], attn, wts["g_attn"], wts["w_out"], *wts["ffn2"],
                 wts["g_final"])
    new_k, new_v = _cache_shift(cache_k, cache_v, k_new, v_new)
    new_conv = jnp.stack([st1, u], axis=1)
    return y.reshape(n, 1, D_MODEL), (new_conv, new_k, new_v)


def kernel(x_prompt, x_sample, state_conv, cache_k, cache_v, norm_ffn1, ffn1_w1, ffn1_w3, ffn1_w2,
           norm_mix, w_in, conv_w, out_norm_conv, out_norm_attn, w_out, norm_ffn2, ffn2_w1,
           ffn2_w3, ffn2_w2, norm_final):
    depth = w_in.shape[0]
    assert depth == 1, "the final norm is fused into the (single) layer's last kernel"
    row = lambda g: g.reshape(1, -1)
    lo = lambda w: w.astype(BF16)
    st_p, st_s = [], []
    yp, ys = x_prompt, x_sample
    for l in range(depth):
        wts = dict(ffn1=(row(norm_ffn1[l]), lo(ffn1_w1[l]), lo(ffn1_w3[l]), lo(ffn1_w2[l])),
                   ffn2=(row(norm_ffn2[l]), lo(ffn2_w1[l]), lo(ffn2_w3[l]), lo(ffn2_w2[l])),
                   g_mix=row(norm_mix[l]), w_in=lo(w_in[l]), conv_w=conv_w[l],
                   g_conv=row(out_norm_conv[l]), g_attn=row(out_norm_attn[l]),
                   w_out=lo(w_out[l]), g_final=row(norm_final))
        yp, sp = _layer_prompt(yp, wts)
        ys, ss = _layer_sample(ys, state_conv[l], cache_k[l], cache_v[l], wts)
        st_p.append(sp)
        st_s.append(ss)
    stack = lambda states, j: jnp.stack([s[j] for s in states], axis=0)
    return (yp, ys, stack(st_p, 0), stack(st_s, 0), stack(st_p, 1), stack(st_p, 2),
            stack(st_s, 1), stack(st_s, 2))
```

```python
import functools

import jax
import jax.numpy as jnp
import numpy as np
from jax import lax
from jax.experimental import pallas as pl
from jax.experimental.pallas import tpu as pltpu

F32 = jnp.float32
BF16 = jnp.bfloat16

D_MODEL = 1024
CONV_CH = D_MODEL // 2
CONV_K = 3
N_HEADS = 8
HEAD_DIM = D_MODEL // 16
ATTN_WIDTH = N_HEADS * HEAD_DIM
D_FF = 2816
ROPE_THETA = 10000.0
EPS = 1e-6
PATTERNS = ((128, 1), (512, 4), (2048, 16))
WIN_MAX = max(w for w, _ in PATTERNS)
PAST_LEN = 8192

LANES = 128
HEADS_PER_TILE = LANES // HEAD_DIM
N_HEAD_TILES = ATTN_WIDTH // LANES
NB = 128
MAX_DIL = max(d for _, d in PATTERNS)
CHUNK_POS = NB * MAX_DIL
ATTN_INTERLEAVE = 4
FF_CHUNK = 256
ROW_TILE = 512
VMEM_LIMIT = 48 * 1024 * 1024

assert all(w // d == NB for w, d in PATTERNS)
assert [d for _, d in PATTERNS] == [1, 4, 16]


def _rms(x, g):
    return x * lax.rsqrt(jnp.mean(x * x, axis=-1, keepdims=True) + EPS) * g


def _dot(a, b):
    return jnp.dot(a, b, preferred_element_type=F32)


def _swiglu(h, w1_ref, w3_ref, w2_ref):
    acc = None
    for c in range(D_FF // FF_CHUNK):
        sl = slice(c * FF_CHUNK, (c + 1) * FF_CHUNK)
        a = _dot(h, w1_ref[:, sl])
        b = _dot(h, w3_ref[:, sl])
        t = (a * jax.nn.sigmoid(a) * b).astype(BF16)
        p = _dot(t, w2_ref[sl, :])
        acc = p if acc is None else acc + p
    return acc


def _const_spec(shape):
    return pl.BlockSpec(shape, lambda *_: (0,) * len(shape))


def _params(n_grid):
    return pltpu.CompilerParams(dimension_semantics=("arbitrary",) * n_grid,
                                vmem_limit_bytes=VMEM_LIMIT)


def _ffn_kernel(x_ref, g_ref, w1_ref, w3_ref, w2_ref, o_ref):
    x = x_ref[...]
    h = _rms(x, g_ref[...]).astype(BF16)
    o_ref[...] = x + 0.5 * _swiglu(h, w1_ref, w3_ref, w2_ref)


def _ffn(x, g, w1, w3, w2):
    m = x.shape[0]
    tm = min(ROW_TILE, m)
    return pl.pallas_call(
        _ffn_kernel,
        out_shape=jax.ShapeDtypeStruct(x.shape, F32),
        grid=(m // tm,),
        in_specs=[pl.BlockSpec((tm, D_MODEL), lambda i: (i, 0)),
                  _const_spec((1, D_MODEL)),
                  _const_spec((D_MODEL, D_FF)), _const_spec((D_MODEL, D_FF)),
                  _const_spec((D_FF, D_MODEL))],
        out_specs=pl.BlockSpec((tm, D_MODEL), lambda i: (i, 0)),
        compiler_params=_params(1),
        name="ffn1",
    )(x, g, w1, w3, w2)


def _rope_tables(pos):
    half = HEAD_DIM // 2
    inv = ROPE_THETA ** (-jnp.arange(half, dtype=F32) * 2.0 / HEAD_DIM)
    ang = pos[:, None] * inv[None, :]
    cos, sin = jnp.cos(ang), jnp.sin(ang)
    cos_t = jnp.concatenate([cos, cos] * HEADS_PER_TILE, axis=-1)
    sin_t = jnp.concatenate([-sin, sin] * HEADS_PER_TILE, axis=-1)
    return cos_t, sin_t


def _rope_tile(x, cos_t, sin_t):
    half = HEAD_DIM // 2
    lane = lax.broadcasted_iota(jnp.int32, x.shape, 1)
    first_half = (lane % HEAD_DIM) < half
    partner = jnp.where(first_half,
                        pltpu.roll(x, LANES - half, axis=1),
                        pltpu.roll(x, half, axis=1))
    return x * cos_t + partner * sin_t


def _mix_in_prompt_kernel(x_ref, g_ref, w_ref, cw_ref, gc_ref, cos_ref, sin_ref,
                          yc_ref, q_ref, k_ref, v_ref, kl_ref, vl_ref, st_ref, carry_ref,
                          *, n_keep_tiles):
    i = pl.program_id(1)
    n_i = pl.num_programs(1)
    tm = x_ref.shape[1]
    w, a = CONV_CH, ATTN_WIDTH
    h = _rms(x_ref[0], g_ref[...]).astype(BF16)

    b_g = _dot(h, w_ref[:, 0:w])
    u = _dot(h, w_ref[:, w:2 * w]) * _dot(h, w_ref[:, 2 * w:3 * w])

    @pl.when(i == 0)
    def _():
        carry_ref[...] = jnp.zeros_like(carry_ref)

    prev2 = carry_ref[0:1, :]
    prev1 = carry_ref[1:2, :]
    row = lax.broadcasted_iota(jnp.int32, (tm, w), 0)
    u1 = jnp.where(row == 0, prev1, pltpu.roll(u, 1, axis=0))
    u2 = jnp.where(row == 0, prev2, jnp.where(row == 1, prev1, pltpu.roll(u, 2, axis=0)))
    conv = cw_ref[0:1, :] * u2 + cw_ref[1:2, :] * u1 + cw_ref[2:3, :] * u
    yc_ref[0] = _rms(b_g * conv, gc_ref[...])
    carry_ref[0:2, :] = u[tm - 2:tm, :]
    st_ref[0] = u[tm - 2:tm, :]

    o = 3 * w
    q = _dot(h, w_ref[:, o:o + a])
    k = _dot(h, w_ref[:, o + a:o + 2 * a])
    v = _dot(h, w_ref[:, o + 2 * a:o + 3 * a])
    cos_t, sin_t = cos_ref[...], sin_ref[...]
    k_tiles = []
    for t in range(N_HEAD_TILES):
        sl = slice(t * LANES, (t + 1) * LANES)
        q_ref[0, t] = _rope_tile(q[:, sl], cos_t, sin_t) * (HEAD_DIM ** -0.5)
        kt = _rope_tile(k[:, sl], cos_t, sin_t)
        k_ref[0, t] = kt
        v_ref[0, t] = v[:, sl]
        k_tiles.append(kt)

    @pl.when(i >= n_i - n_keep_tiles)
    def _():
        kl_ref[0] = jnp.concatenate(k_tiles, axis=1).T
        vl_ref[0] = v.T


def _mix_in_prompt(x, g, w_in, conv_w, g_conv, cos_t, sin_t):
    b, s, _ = x.shape
    tm = min(ROW_TILE, s)
    n_i = s // tm
    keep = min(WIN_MAX, s)
    n_keep = keep // tm
    tile_map = lambda bi, i: (bi, 0, i, 0)
    keep_map = lambda bi, i: (bi, 0, jnp.maximum(i - (n_i - n_keep), 0))
    head_major = jax.ShapeDtypeStruct((b, N_HEAD_TILES, s, LANES), F32)
    return pl.pallas_call(
        functools.partial(_mix_in_prompt_kernel, n_keep_tiles=n_keep),
        out_shape=(jax.ShapeDtypeStruct((b, s, CONV_CH), F32),
                   head_major, head_major, head_major,
                   jax.ShapeDtypeStruct((b, ATTN_WIDTH, keep), F32),
                   jax.ShapeDtypeStruct((b, ATTN_WIDTH, keep), F32),
                   jax.ShapeDtypeStruct((b, CONV_K - 1, CONV_CH), F32)),
        grid=(b, n_i),
        in_specs=[pl.BlockSpec((1, tm, D_MODEL), lambda bi, i: (bi, i, 0)),
                  _const_spec((1, D_MODEL)),
                  _const_spec(w_in.shape),
                  _const_spec((CONV_K, CONV_CH)),
                  _const_spec((1, CONV_CH)),
                  pl.BlockSpec((tm, LANES), lambda bi, i: (i, 0)),
                  pl.BlockSpec((tm, LANES), lambda bi, i: (i, 0))],
        out_specs=(pl.BlockSpec((1, tm, CONV_CH), lambda bi, i: (bi, i, 0)),
                   pl.BlockSpec((1, N_HEAD_TILES, tm, LANES), tile_map),
                   pl.BlockSpec((1, N_HEAD_TILES, tm, LANES), tile_map),
                   pl.BlockSpec((1, N_HEAD_TILES, tm, LANES), tile_map),
                   pl.BlockSpec((1, ATTN_WIDTH, tm), keep_map),
                   pl.BlockSpec((1, ATTN_WIDTH, tm), keep_map),
                   pl.BlockSpec((1, CONV_K - 1, CONV_CH), lambda bi, i: (bi, 0, 0))),
        scratch_shapes=[pltpu.VMEM((8, CONV_CH), F32)],
        compiler_params=_params(2),
        name="mix_in_prompt",
    )(x, g, w_in, conv_w, g_conv, cos_t, sin_t)


def _band_bias():
    out = np.zeros((len(PATTERNS), 2, NB, 2 * NB), np.float32)
    for p, (_, d) in enumerate(PATTERNS):
        g = MAX_DIL // d
        qr, kr = NB // g, 2 * NB // g
        qi = np.arange(NB)
        kj = np.arange(2 * NB)
        qm, qa = qi // qr, qi % qr
        km, ka = kj // kr, kj % kr
        for variant, off in enumerate((NB // g, 0)):
            dist = g * (qa[:, None] - ka[None, :] + off) + (qm[:, None] - km[None, :])
            out[p, variant] = np.where((dist >= 0) & (dist <= NB), 0.0, -np.inf)
    return out.reshape(len(PATTERNS) * 2, NB, 2 * NB)


def _attn_scores(q, k, bias):
    lo = lax.broadcasted_iota(jnp.int32, (NB, LANES), 1) < HEAD_DIM
    kb = k.astype(BF16)
    nt = (((1,), (1,)), ((), ()))
    return [lax.dot_general(qh.astype(BF16), kb, nt, preferred_element_type=F32) + bias
            for qh in (jnp.where(lo, q, 0.0), jnp.where(lo, 0.0, q))]


def _attn_update(scores, v, old):
    lo = lax.broadcasted_iota(jnp.int32, (NB, LANES), 1) < HEAD_DIM
    lo_kv = lax.broadcasted_iota(jnp.int32, (2 * NB, LANES), 1) < HEAD_DIM
    values = (jnp.where(lo_kv, v, 0.0), jnp.where(lo_kv, 0.0, v))
    m_new, row_sum, pv = [], [], None
    for hi, (s, vh) in enumerate(zip(scores, values)):
        row_max = jnp.max(s, axis=-1, keepdims=True)
        if old is None:
            mh = jnp.broadcast_to(row_max, (NB, LANES))
        else:
            mh = jnp.maximum(old[1 + hi], row_max)
        p = jnp.exp(s - jnp.concatenate([mh, mh], axis=1))
        row_sum.append(jnp.sum(p, axis=-1, keepdims=True))
        contrib = _dot(p.astype(BF16), vh.astype(BF16))
        pv = contrib if pv is None else pv + contrib
        m_new.append(mh)
    rs = jnp.where(lo, row_sum[0], row_sum[1])
    if old is None:
        return pv, m_new[0], m_new[1], rs
    alpha = jnp.where(lo, jnp.exp(old[1] - m_new[0]), jnp.exp(old[2] - m_new[1]))
    return alpha * old[0] + pv, m_new[0], m_new[1], alpha * old[3] + rs


def _attn_prompt_kernel(bias_ref, q_ref, k_ref, v_ref, o_ref, acc_ref, m0_ref, m1_ref, l_ref):
    c = pl.program_id(2)
    row0 = c * NB
    state_refs = (acc_ref, m0_ref, m1_ref, l_ref)

    def tile(r):
        return slice(r * LANES, (r + 1) * LANES)

    def load_block(p, classes, a0, with_old):
        qr = NB // len(classes)
        start = row0 + a0 - qr
        variant = (start < 0).astype(jnp.int32)
        ks = pl.multiple_of(jnp.maximum(start, 0), qr)

        def gather(ref, first, rows):
            return jnp.concatenate([ref[0, 0, pl.ds(first, rows), tile(r)] for r in classes], axis=0)

        old = None
        if with_old:
            old = tuple(jnp.concatenate([ref[pl.ds(a0, qr), tile(r)] for r in classes], axis=0)
                        for ref in state_refs)
        return (gather(q_ref, a0, qr), gather(k_ref, ks, 2 * qr), gather(v_ref, ks, 2 * qr),
                bias_ref[2 * p + variant], old)

    def run_group(p, blocks, with_old, final):
        ops = [load_block(p, classes, a0, with_old) for classes, a0 in blocks]
        scores = [_attn_scores(q, k, bias) for q, k, _, bias, _ in ops]
        news = [_attn_update(s, v, old) for s, (_, _, v, _, old) in zip(scores, ops)]
        for (classes, a0), new in zip(blocks, news):
            qr = NB // len(classes)
            if final:
                out = new[0] / new[3]
                for m, r in enumerate(classes):
                    o_ref[0, 0, pl.ds(a0, qr), tile(r)] = out[m * qr:(m + 1) * qr]
            else:
                for ref, val in zip(state_refs, new):
                    for m, r in enumerate(classes):
                        ref[pl.ds(a0, qr), tile(r)] = val[m * qr:(m + 1) * qr]

    for first in range(0, MAX_DIL, ATTN_INTERLEAVE):
        run_group(2, [([r], 0) for r in range(first, first + ATTN_INTERLEAVE)], False, False)

    qr4 = NB // 4
    per4 = ATTN_INTERLEAVE // 4

    def body4(t, carry):
        rows = [pl.multiple_of((t * per4 + u) * qr4, qr4) for u in range(per4)]
        run_group(1, [([4 * m + r4 for m in range(4)], a0) for a0 in rows for r4 in range(4)],
                  True, False)
        return carry

    lax.fori_loop(0, NB // qr4 // per4, body4, 0)

    qr1 = NB // MAX_DIL

    def body1(t, carry):
        rows = [pl.multiple_of((t * ATTN_INTERLEAVE + u) * qr1, qr1) for u in range(ATTN_INTERLEAVE)]
        run_group(0, [(list(range(MAX_DIL)), a0) for a0 in rows], True, True)
        return carry

    lax.fori_loop(0, NB // qr1 // ATTN_INTERLEAVE, body1, 0)


def _attn_prompt(q, k, v):
    b, _, s, _ = q.shape
    assert s % CHUNK_POS == 0
    rows = s // MAX_DIL
    width = MAX_DIL * LANES
    view = lambda t: t.reshape(b, N_HEAD_TILES, rows, width)
    bias = jnp.asarray(_band_bias())
    out = pl.pallas_call(
        _attn_prompt_kernel,
        out_shape=jax.ShapeDtypeStruct((b, N_HEAD_TILES, rows, width), F32),
        grid=(b, N_HEAD_TILES, s // CHUNK_POS),
        in_specs=[_const_spec(bias.shape),
                  pl.BlockSpec((1, 1, NB, width), lambda bi, t, c: (bi, t, c, 0)),
                  pl.BlockSpec((1, 1, rows, width), lambda bi, t, c: (bi, t, 0, 0)),
                  pl.BlockSpec((1, 1, rows, width), lambda bi, t, c: (bi, t, 0, 0))],
        out_specs=pl.BlockSpec((1, 1, NB, width), lambda bi, t, c: (bi, t, c, 0)),
        scratch_shapes=[pltpu.VMEM((NB, width), F32)] * 4,
        compiler_params=_params(3),
        name="attn_prompt",
    )(bias, view(q), view(k), view(v))
    return out.reshape(b, N_HEAD_TILES, s, LANES)


def _mix_out_kernel(x_ref, yc_ref, at_ref, ga_ref, wo_ref, g_ref, w1_ref, w3_ref, w2_ref,
                    gf_ref, o_ref):
    attn = jnp.concatenate([at_ref[0, t] for t in range(N_HEAD_TILES)], axis=1)
    y_a = _rms(attn, ga_ref[...])
    mixed = jnp.concatenate([yc_ref[0], y_a], axis=1).astype(BF16)
    x = x_ref[0] + _dot(mixed, wo_ref[...])
    h = _rms(x, g_ref[...]).astype(BF16)
    x = x + 0.5 * _swiglu(h, w1_ref, w3_ref, w2_ref)
    o_ref[0] = _rms(x, gf_ref[...])


def _mix_out(x, yc, attn, g_attn, w_out, g, w1, w3, w2, g_final):
    b, s, _ = x.shape
    tm = min(ROW_TILE, s)
    row_map = lambda bi, i: (bi, i, 0)
    return pl.pallas_call(
        _mix_out_kernel,
        out_shape=jax.ShapeDtypeStruct(x.shape, F32),
        grid=(b, s // tm),
        in_specs=[pl.BlockSpec((1, tm, D_MODEL), row_map),
                  pl.BlockSpec((1, tm, CONV_CH), row_map),
                  pl.BlockSpec((1, N_HEAD_TILES, tm, LANES), lambda bi, i: (bi, 0, i, 0)),
                  _const_spec((1, ATTN_WIDTH)),
                  _const_spec(w_out.shape),
                  _const_spec((1, D_MODEL)),
                  _const_spec((D_MODEL, D_FF)), _const_spec((D_MODEL, D_FF)),
                  _const_spec((D_FF, D_MODEL)),
                  _const_spec((1, D_MODEL))],
        out_specs=pl.BlockSpec((1, tm, D_MODEL), row_map),
        compiler_params=_params(2),
        name="mix_out_ffn2",
    )(x, yc, attn, g_attn, w_out, g, w1, w3, w2, g_final)


def _mix_in_sample_kernel(x_ref, g_ref, w_ref, cw_ref, gc_ref, cos_ref, sin_ref, s0_ref, s1_ref,
                          yc_ref, q_ref, k_ref, v_ref, u_ref):
    w, a = CONV_CH, ATTN_WIDTH
    h = _rms(x_ref[...], g_ref[...]).astype(BF16)
    b_g = _dot(h, w_ref[:, 0:w])
    u = _dot(h, w_ref[:, w:2 * w]) * _dot(h, w_ref[:, 2 * w:3 * w])
    conv = cw_ref[0:1, :] * s0_ref[...] + cw_ref[1:2, :] * s1_ref[...] + cw_ref[2:3, :] * u
    yc_ref[...] = _rms(b_g * conv, gc_ref[...])
    u_ref[...] = u
    o = 3 * w
    q = _dot(h, w_ref[:, o:o + a])
    k = _dot(h, w_ref[:, o + a:o + 2 * a])
    v_ref[...] = _dot(h, w_ref[:, o + 2 * a:o + 3 * a])
    cos_t, sin_t = cos_ref[...], sin_ref[...]
    for t in range(N_HEAD_TILES):
        sl = slice(t * LANES, (t + 1) * LANES)
        q_ref[:, sl] = _rope_tile(q[:, sl], cos_t, sin_t) * (HEAD_DIM ** -0.5)
        k_ref[:, sl] = _rope_tile(k[:, sl], cos_t, sin_t)


def _mix_in_sample(x, g, w_in, conv_w, g_conv, cos_t, sin_t, st0, st1):
    n = x.shape[0]
    vec = lambda width: jax.ShapeDtypeStruct((n, width), F32)
    return pl.pallas_call(
        _mix_in_sample_kernel,
        out_shape=(vec(CONV_CH), vec(ATTN_WIDTH), vec(ATTN_WIDTH), vec(ATTN_WIDTH), vec(CONV_CH)),
        compiler_params=pltpu.CompilerParams(vmem_limit_bytes=VMEM_LIMIT),
        name="mix_in_sample",
    )(x, g, w_in, conv_w, g_conv, cos_t, sin_t, st0, st1)


def _sample_masks(w):
    j = w - np.arange(w)
    mult = sum(((j % d == 0) & (j <= win)).astype(np.float32) for win, d in PATTERNS)
    bias = np.where(mult > 0, 0.0, -np.inf).astype(np.float32)
    return bias.reshape(1, w), mult.reshape(1, w)


def _attn_shift_sample_kernel(q_ref, kn_ref, vn_ref, bias_ref, mult_ref, ck_ref, cv_ref,
                              o_ref, ok_ref, ov_ref):
    w = ck_ref.shape[-1]
    n_pat = float(len(PATTERNS))
    qt, knt, vnt = q_ref[0], kn_ref[0], vn_ref[0]
    s_self_all = jnp.sum(qt * knt, axis=0, keepdims=True)
    newest = lax.broadcasted_iota(jnp.int32, (HEAD_DIM, w), 1) == w - 1
    head_lane = lax.broadcasted_iota(jnp.int32, (HEAD_DIM, LANES), 1)
    out = jnp.zeros((HEAD_DIM, LANES), F32)
    for h in range(N_HEADS):
        col = slice(h, h + 1)
        kh, vh = ck_ref[0, h], cv_ref[0, h]
        s = jnp.sum(kh * qt[:, col], axis=0, keepdims=True) + bias_ref[...]
        s_self = s_self_all[:, col]
        m = jnp.maximum(jnp.max(s, axis=-1, keepdims=True), s_self)
        e = jnp.exp(s - m) * mult_ref[...]
        e_self = n_pat * jnp.exp(s_self - m)
        den = jnp.sum(e, axis=-1, keepdims=True) + e_self
        weighted = vh * e
        part = weighted[:, 0:LANES]
        for t in range(1, w // LANES):
            part = part + weighted[:, t * LANES:(t + 1) * LANES]
        num = jnp.sum(part, axis=-1, keepdims=True) + e_self * vnt[:, col]
        out = jnp.where(head_lane == h, num / den, out)
        ok_ref[0, h] = jnp.where(newest, knt[:, col], pltpu.roll(kh, w - 1, axis=1))
        ov_ref[0, h] = jnp.where(newest, vnt[:, col], pltpu.roll(vh, w - 1, axis=1))
    o_ref[0] = out


def _attn_shift_sample(q, k_new, v_new, cache_k, cache_v):
    n, _, _, w = cache_k.shape
    assert w == WIN_MAX, "every window must lie inside the cached range"
    bias, mult = (jnp.asarray(a) for a in _sample_masks(w))
    vec_spec = pl.BlockSpec((1, HEAD_DIM, N_HEADS), lambda b: (b, 0, 0))
    cache_spec = pl.BlockSpec((1, N_HEADS, HEAD_DIM, w), lambda b: (b, 0, 0, 0))
    cache_shape = jax.ShapeDtypeStruct(cache_k.shape, cache_k.dtype)
    return pl.pallas_call(
        _attn_shift_sample_kernel,
        out_shape=(jax.ShapeDtypeStruct((n, HEAD_DIM, LANES), F32), cache_shape, cache_shape),
        grid=(n,),
        in_specs=[vec_spec, vec_spec, vec_spec, _const_spec((1, w)), _const_spec((1, w)),
                  cache_spec, cache_spec],
        out_specs=(pl.BlockSpec((1, HEAD_DIM, LANES), lambda b: (b, 0, 0)), cache_spec, cache_spec),
        compiler_params=_params(1),
        name="attn_shift_sample",
    )(q, k_new, v_new, bias, mult, cache_k, cache_v)


def _layer_prompt(x, wts):
    b, s, _ = x.shape
    cos_t, sin_t = _rope_tables(jnp.arange(s, dtype=F32))
    x1 = _ffn(x.reshape(b * s, D_MODEL), *wts["ffn1"]).reshape(b, s, D_MODEL)
    yc, q, k, v, kt_last, vt_last, u_state = _mix_in_prompt(
        x1, wts["g_mix"], wts["w_in"], wts["conv_w"], wts["g_conv"], cos_t, sin_t)
    attn = _attn_prompt(q, k, v)
    y = _mix_out(x1, yc, attn, wts["g_attn"], wts["w_out"], *wts["ffn2"], wts["g_final"])
    to_cache = lambda t: t.reshape(b, N_HEADS, HEAD_DIM, -1).transpose(0, 3, 1, 2)
    return y, (u_state, to_cache(kt_last), to_cache(vt_last))


def _layer_sample(x, conv_state, cache_k, cache_v, wts):
    n = x.shape[0]
    assert x.shape[1] == 1, "one new token per cached sequence"
    cos_t, sin_t = _rope_tables(jnp.full((1,), PAST_LEN, dtype=F32))
    x1 = _ffn(x.reshape(n, D_MODEL), *wts["ffn1"])
    st0, st1 = conv_state[:, 0], conv_state[:, 1]
    yc, q, k, v, u = _mix_in_sample(x1, wts["g_mix"], wts["w_in"], wts["conv_w"], wts["g_conv"],
                                    cos_t, sin_t, st0, st1)
    dh_head = lambda t: t.reshape(n, N_HEADS, HEAD_DIM).transpose(0, 2, 1)
    pos_minor = lambda c: c.transpose(0, 2, 3, 1)
    attn, new_k, new_v = _attn_shift_sample(dh_head(q), dh_head(k), dh_head(v),
                                            pos_minor(cache_k), pos_minor(cache_v))
    attn = attn[:, :, :N_HEADS].transpose(0, 2, 1).reshape(n, N_HEAD_TILES, LANES)
    attn = attn.transpose(1, 0, 2)[None]
    y = _mix_out(x1[None], yc[None], attn, wts["g_attn"], wts["w_out"], *wts["ffn2"],
                 wts["g_final"])
    new_conv = jnp.stack([st1, u], axis=1)
    pos_major = lambda c: c.transpose(0, 3, 1, 2)
    return y.reshape(n, 1, D_MODEL), (new_conv, pos_major(new_k), pos_major(new_v))


def kernel(x_prompt, x_sample, state_conv, cache_k, cache_v, norm_ffn1, ffn1_w1, ffn1_w3, ffn1_w2,
           norm_mix, w_in, conv_w, out_norm_conv, out_norm_attn, w_out, norm_ffn2, ffn2_w1,
           ffn2_w3, ffn2_w2, norm_final):
    depth = w_in.shape[0]
    assert depth == 1, "the final norm is fused into the (single) layer's last kernel"
    row = lambda g: g.reshape(1, -1)
    lo = lambda w: w.astype(BF16)
    st_p, st_s = [], []
    yp, ys = x_prompt, x_sample
    for l in range(depth):
        wts = dict(ffn1=(row(norm_ffn1[l]), lo(ffn1_w1[l]), lo(ffn1_w3[l]), lo(ffn1_w2[l])),
                   ffn2=(row(norm_ffn2[l]), lo(ffn2_w1[l]), lo(ffn2_w3[l]), lo(ffn2_w2[l])),
                   g_mix=row(norm_mix[l]), w_in=lo(w_in[l]), conv_w=conv_w[l],
                   g_conv=row(out_norm_conv[l]), g_attn=row(out_norm_attn[l]),
                   w_out=lo(w_out[l]), g_final=row(norm_final))
        yp, sp = _layer_prompt(yp, wts)
        ys, ss = _layer_sample(ys, state_conv[l], cache_k[l], cache_v[l], wts)
        st_p.append(sp)
        st_s.append(ss)
    stack = lambda states, j: jnp.stack([s[j] for s in states], axis=0)
    return (yp, ys, stack(st_p, 0), stack(st_s, 0), stack(st_p, 1), stack(st_p, 2),
            stack(st_s, 1), stack(st_s, 2))
```

```python
import functools

import jax
import jax.numpy as jnp
import numpy as np
from jax import lax
from jax.experimental import pallas as pl
from jax.experimental.pallas import tpu as pltpu

F32 = jnp.float32
BF16 = jnp.bfloat16

D_MODEL = 1024
CONV_CH = D_MODEL // 2
CONV_K = 3
N_HEADS = 8
HEAD_DIM = D_MODEL // 16
ATTN_WIDTH = N_HEADS * HEAD_DIM
D_FF = 2816
ROPE_THETA = 10000.0
EPS = 1e-6
LOG2_E = 1.4426950408889634
PATTERNS = ((128, 1), (512, 4), (2048, 16))
WIN_MAX = max(w for w, _ in PATTERNS)
PAST_LEN = 8192

LANES = 128
HEADS_PER_TILE = LANES // HEAD_DIM
N_HEAD_TILES = ATTN_WIDTH // LANES
NB = 128
MAX_DIL = max(d for _, d in PATTERNS)
CHUNK_POS = NB * MAX_DIL
ATTN_INTERLEAVE = 4
FF_CHUNK = 256
ROW_TILE = 512
VMEM_LIMIT = 56 * 1024 * 1024

assert all(w // d == NB for w, d in PATTERNS)
assert [d for _, d in PATTERNS] == [1, 4, 16]


def _rms(x, g):
    return x * lax.rsqrt(jnp.mean(x * x, axis=-1, keepdims=True) + EPS) * g


def _dot(a, b):
    return jnp.dot(a, b, preferred_element_type=F32)


def _swiglu(h, w1_ref, w3_ref, w2_ref):
    acc = None
    for c in range(D_FF // FF_CHUNK):
        sl = slice(c * FF_CHUNK, (c + 1) * FF_CHUNK)
        a = _dot(h, w1_ref[:, sl])
        b = _dot(h, w3_ref[:, sl])
        t = (a * jax.nn.sigmoid(a) * b).astype(BF16)
        p = _dot(t, w2_ref[sl, :])
        acc = p if acc is None else acc + p
    return acc


def _const_spec(shape):
    return pl.BlockSpec(shape, lambda *_: (0,) * len(shape), pipeline_mode=pl.Buffered(1))


def _params(n_grid):
    return pltpu.CompilerParams(dimension_semantics=("arbitrary",) * n_grid,
                                vmem_limit_bytes=VMEM_LIMIT)


def _ffn_kernel(x_ref, g_ref, w1_ref, w3_ref, w2_ref, *rest):
    o_ref = rest[-1] if len(rest) == 1 else rest[5]
    if len(rest) > 1:
        _sample_key_pass(*rest[:5], *rest[6:])
    x = x_ref[...]
    h = _rms(x, g_ref[...]).astype(BF16)
    o_ref[...] = x + 0.5 * _swiglu(h, w1_ref, w3_ref, w2_ref)


def _ffn(x, g, w1, w3, w2, sample=None):
    m = x.shape[0]
    tm = min(ROW_TILE, m)
    n_steps = m // tm
    in_specs = [pl.BlockSpec((tm, D_MODEL), lambda i: (i, 0)),
                _const_spec((1, D_MODEL)),
                _const_spec((D_MODEL, D_FF)), _const_spec((D_MODEL, D_FF)),
                _const_spec((D_FF, D_MODEL))]
    out_shape = [jax.ShapeDtypeStruct(x.shape, F32)]
    out_specs = [pl.BlockSpec((tm, D_MODEL), lambda i: (i, 0))]
    args = [x, g, w1, w3, w2]
    if sample is not None:
        q, k_new, cache_k = sample
        n, _, _, w = cache_k.shape
        assert n == n_steps, "one cached sequence per row tile"
        bias, mult = (jnp.asarray(a) for a in _sample_masks(w))
        seq = lambda *dims: pl.BlockSpec((1,) + dims, lambda i: (i,) + (0,) * len(dims))
        in_specs += [seq(HEAD_DIM, N_HEADS), seq(HEAD_DIM, N_HEADS), _const_spec((1, w)),
                     _const_spec((1, w)), seq(N_HEADS, HEAD_DIM, w)]
        args += [q, k_new, bias, mult, cache_k]
        out_shape += [jax.ShapeDtypeStruct((n, N_HEADS, w), F32),
                      jax.ShapeDtypeStruct((n, N_HEADS, LANES), F32),
                      jax.ShapeDtypeStruct(cache_k.shape, cache_k.dtype)]
        out_specs += [seq(N_HEADS, w), seq(N_HEADS, LANES), seq(N_HEADS, HEAD_DIM, w)]
    out = pl.pallas_call(
        _ffn_kernel,
        out_shape=tuple(out_shape),
        grid=(n_steps,),
        in_specs=in_specs,
        out_specs=tuple(out_specs),
        compiler_params=_params(1),
        name="ffn1",
    )(*args)
    return out[0] if sample is None else out


def _rope_tables(pos):
    half = HEAD_DIM // 2
    inv = ROPE_THETA ** (-jnp.arange(half, dtype=F32) * 2.0 / HEAD_DIM)
    ang = pos[:, None] * inv[None, :]
    cos, sin = jnp.cos(ang), jnp.sin(ang)
    cos_t = jnp.concatenate([cos, cos] * HEADS_PER_TILE, axis=-1)
    sin_t = jnp.concatenate([-sin, sin] * HEADS_PER_TILE, axis=-1)
    return cos_t, sin_t


def _rope_tile(x, cos_t, sin_t):
    half = HEAD_DIM // 2
    lane = lax.broadcasted_iota(jnp.int32, x.shape, 1)
    first_half = (lane % HEAD_DIM) < half
    partner = jnp.where(first_half,
                        pltpu.roll(x, LANES - half, axis=1),
                        pltpu.roll(x, half, axis=1))
    return x * cos_t + partner * sin_t


def _mix_in_prompt_kernel(x_ref, g_ref, w_ref, cw_ref, gc_ref, cos_ref, sin_ref,
                          yc_ref, q_ref, k_ref, v_ref, kl_ref, vl_ref, st_ref, carry_ref,
                          *, n_keep_tiles):
    i = pl.program_id(1)
    n_i = pl.num_programs(1)
    tm = x_ref.shape[1]
    w, a = CONV_CH, ATTN_WIDTH
    h = _rms(x_ref[0], g_ref[...]).astype(BF16)

    b_g = _dot(h, w_ref[:, 0:w])
    u = _dot(h, w_ref[:, w:2 * w]) * _dot(h, w_ref[:, 2 * w:3 * w])

    @pl.when(i == 0)
    def _():
        carry_ref[...] = jnp.zeros_like(carry_ref)

    prev2 = carry_ref[0:1, :]
    prev1 = carry_ref[1:2, :]
    row = lax.broadcasted_iota(jnp.int32, (tm, w), 0)
    u1 = jnp.where(row == 0, prev1, pltpu.roll(u, 1, axis=0))
    u2 = jnp.where(row == 0, prev2, jnp.where(row == 1, prev1, pltpu.roll(u, 2, axis=0)))
    conv = cw_ref[0:1, :] * u2 + cw_ref[1:2, :] * u1 + cw_ref[2:3, :] * u
    yc_ref[0] = _rms(b_g * conv, gc_ref[...])
    carry_ref[0:2, :] = u[tm - 2:tm, :]
    st_ref[0] = u[tm - 2:tm, :]

    o = 3 * w
    q = _dot(h, w_ref[:, o:o + a])
    k = _dot(h, w_ref[:, o + a:o + 2 * a])
    v = _dot(h, w_ref[:, o + 2 * a:o + 3 * a])
    cos_t, sin_t = cos_ref[...], sin_ref[...]
    k_tiles = []
    for t in range(N_HEAD_TILES):
        sl = slice(t * LANES, (t + 1) * LANES)
        q_ref[0, t] = _rope_tile(q[:, sl], cos_t, sin_t) * (HEAD_DIM ** -0.5 * LOG2_E)
        kt = _rope_tile(k[:, sl], cos_t, sin_t)
        k_ref[0, t] = kt
        v_ref[0, t] = v[:, sl]
        k_tiles.append(kt)

    @pl.when(i >= n_i - n_keep_tiles)
    def _():
        kl_ref[0] = jnp.concatenate(k_tiles, axis=1).T
        vl_ref[0] = v.T


def _mix_in_prompt(x, g, w_in, conv_w, g_conv, cos_t, sin_t):
    b, s, _ = x.shape
    tm = min(ROW_TILE, s)
    n_i = s // tm
    keep = min(WIN_MAX, s)
    n_keep = keep // tm
    tile_map = lambda bi, i: (bi, 0, i, 0)
    keep_map = lambda bi, i: (bi, 0, jnp.maximum(i - (n_i - n_keep), 0))
    head_major = jax.ShapeDtypeStruct((b, N_HEAD_TILES, s, LANES), F32)
    return pl.pallas_call(
        functools.partial(_mix_in_prompt_kernel, n_keep_tiles=n_keep),
        out_shape=(jax.ShapeDtypeStruct((b, s, CONV_CH), F32),
                   head_major, head_major, head_major,
                   jax.ShapeDtypeStruct((b, ATTN_WIDTH, keep), F32),
                   jax.ShapeDtypeStruct((b, ATTN_WIDTH, keep), F32),
                   jax.ShapeDtypeStruct((b, CONV_K - 1, CONV_CH), F32)),
        grid=(b, n_i),
        in_specs=[pl.BlockSpec((1, tm, D_MODEL), lambda bi, i: (bi, i, 0)),
                  _const_spec((1, D_MODEL)),
                  _const_spec(w_in.shape),
                  _const_spec((CONV_K, CONV_CH)),
                  _const_spec((1, CONV_CH)),
                  pl.BlockSpec((tm, LANES), lambda bi, i: (i, 0)),
                  pl.BlockSpec((tm, LANES), lambda bi, i: (i, 0))],
        out_specs=(pl.BlockSpec((1, tm, CONV_CH), lambda bi, i: (bi, i, 0)),
                   pl.BlockSpec((1, N_HEAD_TILES, tm, LANES), tile_map),
                   pl.BlockSpec((1, N_HEAD_TILES, tm, LANES), tile_map),
                   pl.BlockSpec((1, N_HEAD_TILES, tm, LANES), tile_map),
                   pl.BlockSpec((1, ATTN_WIDTH, tm), keep_map),
                   pl.BlockSpec((1, ATTN_WIDTH, tm), keep_map),
                   pl.BlockSpec((1, CONV_K - 1, CONV_CH), lambda bi, i: (bi, 0, 0))),
        scratch_shapes=[pltpu.VMEM((8, CONV_CH), F32)],
        compiler_params=_params(2),
        name="mix_in_prompt",
    )(x, g, w_in, conv_w, g_conv, cos_t, sin_t)


def _band_bias():
    dist = np.arange(NB)[:, None] - np.arange(2 * NB)[None, :]
    out = [np.where((dist + off >= 0) & (dist + off <= NB), 0.0, -np.inf) for off in (NB, 0)]
    return np.stack(out).astype(np.float32)


def _attn_scores(q, k, bias):
    lo = lax.broadcasted_iota(jnp.int32, (NB, LANES), 1) < HEAD_DIM
    kb = k.astype(BF16)
    nt = (((1,), (1,)), ((), ()))
    return [lax.dot_general(qh.astype(BF16), kb, nt, preferred_element_type=F32) + bias
            for qh in (jnp.where(lo, q, 0.0), jnp.where(lo, 0.0, q))]


def _attn_update(scores, v, old):
    lo = lax.broadcasted_iota(jnp.int32, (NB, LANES), 1) < HEAD_DIM
    vb = v.astype(BF16)
    m_new, row_sum, pv = [], [], []
    for hi, s in enumerate(scores):
        row_max = jnp.max(s, axis=-1, keepdims=True)
        if old is None:
            mh = jnp.broadcast_to(row_max, (NB, LANES))
        else:
            mh = jnp.maximum(old[1 + hi], row_max)
        p = jnp.exp2(s - jnp.concatenate([mh, mh], axis=1))
        row_sum.append(jnp.sum(p, axis=-1, keepdims=True))
        pv.append(_dot(p.astype(BF16), vb))
        m_new.append(mh)
    pv = jnp.where(lo, pv[0], pv[1])
    rs = jnp.where(lo, row_sum[0], row_sum[1])
    if old is None:
        return pv, m_new[0], m_new[1], rs
    alpha = jnp.where(lo, jnp.exp2(old[1] - m_new[0]), jnp.exp2(old[2] - m_new[1]))
    return alpha * old[0] + pv, m_new[0], m_new[1], alpha * old[3] + rs


def _attn_prompt_kernel(bias_ref, q_ref, k_ref, v_ref, o_ref, acc_ref, m0_ref, m1_ref, l_ref):
    pos0 = pl.program_id(2) * CHUNK_POS
    state_refs = (acc_ref, m0_ref, m1_ref, l_ref)

    def rows(start, n, d):
        return pl.ds(start, n) if d == 1 else pl.ds(start, n, stride=d)

    def load_block(d, q0, with_old):
        start = pos0 + q0 - NB * d
        first = start < 0
        ks = jnp.where(first, start + NB * d, start)
        if d == 1:
            ks = pl.multiple_of(ks, NB)
        old = tuple(ref[rows(q0, NB, d), :] for ref in state_refs) if with_old else None
        return (q_ref[0, 0, rows(q0, NB, d), :], k_ref[0, 0, rows(ks, 2 * NB, d), :],
                v_ref[0, 0, rows(ks, 2 * NB, d), :], bias_ref[first.astype(jnp.int32)], old)

    def run_group(d, starts, with_old, final):
        ops = [load_block(d, q0, with_old) for q0 in starts]
        scores = [_attn_scores(q, k, bias) for q, k, _, bias, _ in ops]
        news = [_attn_update(s, v, old) for s, (_, _, v, _, old) in zip(scores, ops)]
        for q0, new in zip(starts, news):
            if final:
                o_ref[0, 0, rows(q0, NB, d), :] = new[0] / new[3]
            else:
                for ref, val in zip(state_refs, new):
                    ref[rows(q0, NB, d), :] = val

    def run_pattern(d, with_old, final):
        per_seg = min(d, ATTN_INTERLEAVE)
        segs_per_trip = ATTN_INTERLEAVE // per_seg
        n_seg = CHUNK_POS // (NB * d)
        for r0 in range(0, d, per_seg):
            if n_seg == segs_per_trip:
                run_group(d, [sg * NB * d + r0 + r for sg in range(n_seg) for r in range(per_seg)],
                          with_old, final)
                continue

            def body(t, carry, r0=r0):
                base = t * (segs_per_trip * NB * d)
                if d == 1:
                    base = pl.multiple_of(base, NB)
                run_group(d, [base + sg * NB * d + r0 + r
                              for sg in range(segs_per_trip) for r in range(per_seg)],
                          with_old, final)
                return carry

            lax.fori_loop(0, n_seg // segs_per_trip, body, 0)

    run_pattern(16, False, False)
    run_pattern(4, True, False)
    run_pattern(1, True, True)


def _attn_prompt(q, k, v):
    b, _, s, _ = q.shape
    assert s % CHUNK_POS == 0
    bias = jnp.asarray(_band_bias())
    chunk_spec = pl.BlockSpec((1, 1, CHUNK_POS, LANES), lambda bi, t, c: (bi, t, c, 0))
    seq_spec = pl.BlockSpec((1, 1, s, LANES), lambda bi, t, c: (bi, t, 0, 0))
    return pl.pallas_call(
        _attn_prompt_kernel,
        out_shape=jax.ShapeDtypeStruct(q.shape, F32),
        grid=(b, N_HEAD_TILES, s // CHUNK_POS),
        in_specs=[_const_spec(bias.shape), chunk_spec, seq_spec, seq_spec],
        out_specs=chunk_spec,
        scratch_shapes=[pltpu.VMEM((CHUNK_POS, LANES), F32)] * 4,
        compiler_params=_params(3),
        name="attn_prompt",
    )(bias, q, k, v)


def _mix_out_kernel(x_ref, yc_ref, at_ref, ga_ref, wo_ref, g_ref, w1_ref, w3_ref, w2_ref,
                    gf_ref, *rest):
    o_ref = rest[-1] if len(rest) == 1 else rest[4]
    if len(rest) > 1:
        _sample_value_pass(*rest[:4], *rest[5:])
    attn = jnp.concatenate([at_ref[0, t] for t in range(N_HEAD_TILES)], axis=1)
    y_a = _rms(attn, ga_ref[...])
    mixed = jnp.concatenate([yc_ref[0], y_a], axis=1).astype(BF16)
    x = x_ref[0] + _dot(mixed, wo_ref[...])
    h = _rms(x, g_ref[...]).astype(BF16)
    x = x + 0.5 * _swiglu(h, w1_ref, w3_ref, w2_ref)
    o_ref[0] = _rms(x, gf_ref[...])


def _mix_out(x, yc, attn, g_attn, w_out, g, w1, w3, w2, g_final, sample=None):
    b, s, _ = x.shape
    tm = min(ROW_TILE, s)
    n_i = s // tm
    row_map = lambda bi, i: (bi, i, 0)
    in_specs = [pl.BlockSpec((1, tm, D_MODEL), row_map),
                pl.BlockSpec((1, tm, CONV_CH), row_map),
                pl.BlockSpec((1, N_HEAD_TILES, tm, LANES), lambda bi, i: (bi, 0, i, 0)),
                _const_spec((1, ATTN_WIDTH)),
                _const_spec(w_out.shape),
                _const_spec((1, D_MODEL)),
                _const_spec((D_MODEL, D_FF)), _const_spec((D_MODEL, D_FF)),
                _const_spec((D_FF, D_MODEL)),
                _const_spec((1, D_MODEL))]
    out_shape = [jax.ShapeDtypeStruct(x.shape, F32)]
    out_specs = [pl.BlockSpec((1, tm, D_MODEL), row_map)]
    args = [x, yc, attn, g_attn, w_out, g, w1, w3, w2, g_final]
    if sample is not None:
        weights, self_weight, v_new, cache_v = sample
        n, _, _, w = cache_v.shape
        assert n == b * n_i, "one cached sequence per row tile"
        seq = lambda *dims: pl.BlockSpec((1,) + dims,
                                         lambda bi, i: (bi * n_i + i,) + (0,) * len(dims))
        in_specs += [seq(N_HEADS, w), seq(N_HEADS, LANES), seq(HEAD_DIM, N_HEADS),
                     seq(N_HEADS, HEAD_DIM, w)]
        args += [weights, self_weight, v_new, cache_v]
        out_shape += [jax.ShapeDtypeStruct((n, HEAD_DIM, LANES), F32),
                      jax.ShapeDtypeStruct(cache_v.shape, cache_v.dtype)]
        out_specs += [seq(HEAD_DIM, LANES), seq(N_HEADS, HEAD_DIM, w)]
    out = pl.pallas_call(
        _mix_out_kernel,
        out_shape=tuple(out_shape),
        grid=(b, n_i),
        in_specs=in_specs,
        out_specs=tuple(out_specs),
        compiler_params=_params(2),
        name="mix_out_ffn2",
    )(*args)
    return out[0] if sample is None else out


def _mix_in_sample_kernel(x_ref, g_ref, w_ref, cw_ref, gc_ref, cos_ref, sin_ref, s0_ref, s1_ref,
                          yc_ref, q_ref, k_ref, v_ref, u_ref):
    w, a = CONV_CH, ATTN_WIDTH
    h = _rms(x_ref[...], g_ref[...]).astype(BF16)
    b_g = _dot(h, w_ref[:, 0:w])
    u = _dot(h, w_ref[:, w:2 * w]) * _dot(h, w_ref[:, 2 * w:3 * w])
    conv = cw_ref[0:1, :] * s0_ref[...] + cw_ref[1:2, :] * s1_ref[...] + cw_ref[2:3, :] * u
    yc_ref[...] = _rms(b_g * conv, gc_ref[...])
    u_ref[...] = u
    o = 3 * w
    q = _dot(h, w_ref[:, o:o + a])
    k = _dot(h, w_ref[:, o + a:o + 2 * a])
    v_ref[...] = _dot(h, w_ref[:, o + 2 * a:o + 3 * a])
    cos_t, sin_t = cos_ref[...], sin_ref[...]
    for t in range(N_HEAD_TILES):
        sl = slice(t * LANES, (t + 1) * LANES)
        q_ref[:, sl] = _rope_tile(q[:, sl], cos_t, sin_t) * (HEAD_DIM ** -0.5)
        k_ref[:, sl] = _rope_tile(k[:, sl], cos_t, sin_t)


def _mix_in_sample(x, g, w_in, conv_w, g_conv, cos_t, sin_t, st0, st1):
    n = x.shape[0]
    vec = lambda width: jax.ShapeDtypeStruct((n, width), F32)
    return pl.pallas_call(
        _mix_in_sample_kernel,
        out_shape=(vec(CONV_CH), vec(ATTN_WIDTH), vec(ATTN_WIDTH), vec(ATTN_WIDTH), vec(CONV_CH)),
        compiler_params=pltpu.CompilerParams(vmem_limit_bytes=VMEM_LIMIT),
        name="mix_in_sample",
    )(x, g, w_in, conv_w, g_conv, cos_t, sin_t, st0, st1)


def _sample_masks(w):
    j = w - np.arange(w)
    mult = sum(((j % d == 0) & (j <= win)).astype(np.float32) for win, d in PATTERNS)
    bias = np.where(mult > 0, 0.0, -np.inf).astype(np.float32)
    return bias.reshape(1, w), mult.reshape(1, w)


def _lane_tile_sum(x):
    part = x[:, 0:LANES]
    for t in range(1, x.shape[1] // LANES):
        part = part + x[:, t * LANES:(t + 1) * LANES]
    return part


def _shifted(cache, new_col):
    w = cache.shape[-1]
    newest = lax.broadcasted_iota(jnp.int32, cache.shape, 1) == w - 1
    return jnp.where(newest, new_col, pltpu.roll(cache, w - 1, axis=1))


def _sample_key_pass(q_ref, kn_ref, bias_ref, mult_ref, ck_ref, p_ref, ps_ref, ok_ref):
    n_pat = float(len(PATTERNS))
    qt, knt = q_ref[0], kn_ref[0]
    s_self_all = jnp.sum(qt * knt, axis=0, keepdims=True)
    for h in range(N_HEADS):
        col = slice(h, h + 1)
        kh = ck_ref[0, h]
        s = jnp.sum(kh * qt[:, col], axis=0, keepdims=True) + bias_ref[...]
        s_self = s_self_all[:, col]
        m = jnp.maximum(jnp.max(s, axis=-1, keepdims=True), s_self)
        e = jnp.exp(s - m) * mult_ref[...]
        e_self = n_pat * jnp.exp(s_self - m)
        den = jnp.sum(_lane_tile_sum(e), axis=-1, keepdims=True) + e_self
        p_ref[0, col, :] = e / den
        ps_ref[0, col, :] = jnp.broadcast_to(e_self / den, (1, LANES))
        ok_ref[0, h] = _shifted(kh, knt[:, col])


def _sample_value_pass(p_ref, ps_ref, vn_ref, cv_ref, o_ref, ov_ref):
    vnt = vn_ref[0]
    head_lane = lax.broadcasted_iota(jnp.int32, (HEAD_DIM, LANES), 1)
    out = jnp.zeros((HEAD_DIM, LANES), F32)
    for h in range(N_HEADS):
        col = slice(h, h + 1)
        vh = cv_ref[0, h]
        num = jnp.sum(_lane_tile_sum(vh * p_ref[0, col, :]), axis=-1, keepdims=True)
        num = num + ps_ref[0, col, 0:1] * vnt[:, col]
        out = jnp.where(head_lane == h, num, out)
        ov_ref[0, h] = _shifted(vh, vnt[:, col])
    o_ref[0] = out


def _layer(xp, xs, conv_state, cache_k, cache_v, wts):
    b, s, _ = xp.shape
    n = xs.shape[0]
    assert xs.shape[1] == 1, "one new token per cached sequence"
    assert cache_k.shape[1] == WIN_MAX, "every window must lie inside the cached range"
    mix = (wts["g_mix"], wts["w_in"], wts["conv_w"], wts["g_conv"])
    tail = (wts["g_attn"], wts["w_out"], *wts["ffn2"], wts["g_final"])

    x1s = _ffn(xs.reshape(n, D_MODEL), *wts["ffn1"])
    st0, st1 = conv_state[:, 0], conv_state[:, 1]
    ycs, qs, ks, vs, us = _mix_in_sample(x1s, *mix, *_rope_tables(jnp.full((1,), PAST_LEN, F32)),
                                         st0, st1)
    dh_head = lambda t: t.reshape(n, N_HEADS, HEAD_DIM).transpose(0, 2, 1)
    pos_minor = lambda c: c.transpose(0, 2, 3, 1)
    pos_major = lambda c: c.transpose(0, 3, 1, 2)

    x1, weights, self_weight, new_k = _ffn(xp.reshape(b * s, D_MODEL), *wts["ffn1"],
                                           sample=(dh_head(qs), dh_head(ks), pos_minor(cache_k)))
    x1 = x1.reshape(b, s, D_MODEL)
    yc, q, k, v, kt_last, vt_last, u_state = _mix_in_prompt(
        x1, *mix, *_rope_tables(jnp.arange(s, dtype=F32)))
    attn = _attn_prompt(q, k, v)
    yp, attn_s, new_v = _mix_out(x1, yc, attn, *tail,
                                 sample=(weights, self_weight, dh_head(vs), pos_minor(cache_v)))
    to_cache = lambda t: pos_major(t.reshape(b, N_HEADS, HEAD_DIM, -1))
    state_p = (u_state, to_cache(kt_last), to_cache(vt_last))

    attn_s = attn_s[:, :, :N_HEADS].transpose(0, 2, 1).reshape(n, N_HEAD_TILES, LANES)
    attn_s = attn_s.transpose(1, 0, 2)[None]
    ys = _mix_out(x1s[None], ycs[None], attn_s, *tail).reshape(n, 1, D_MODEL)
    state_s = (jnp.stack([st1, us], axis=1), pos_major(new_k), pos_major(new_v))
    return yp, ys, state_p, state_s


def kernel(x_prompt, x_sample, state_conv, cache_k, cache_v, norm_ffn1, ffn1_w1, ffn1_w3, ffn1_w2,
           norm_mix, w_in, conv_w, out_norm_conv, out_norm_attn, w_out, norm_ffn2, ffn2_w1,
           ffn2_w3, ffn2_w2, norm_final):
    depth = w_in.shape[0]
    assert depth == 1, "the final norm is fused into the (single) layer's last kernel"
    row = lambda g: g.reshape(1, -1)
    lo = lambda w: w.astype(BF16)
    st_p, st_s = [], []
    yp, ys = x_prompt, x_sample
    for l in range(depth):
        wts = dict(ffn1=(row(norm_ffn1[l]), lo(ffn1_w1[l]), lo(ffn1_w3[l]), lo(ffn1_w2[l])),
                   ffn2=(row(norm_ffn2[l]), lo(ffn2_w1[l]), lo(ffn2_w3[l]), lo(ffn2_w2[l])),
                   g_mix=row(norm_mix[l]), w_in=lo(w_in[l]), conv_w=conv_w[l],
                   g_conv=row(out_norm_conv[l]), g_attn=row(out_norm_attn[l]),
                   w_out=lo(w_out[l]), g_final=row(norm_final))
        yp, ys, sp, ss = _layer(yp, ys, state_conv[l], cache_k[l], cache_v[l], wts)
        st_p.append(sp)
        st_s.append(ss)
    stack = lambda states, j: jnp.stack([s[j] for s in states], axis=0)
    return (yp, ys, stack(st_p, 0), stack(st_s, 0), stack(st_p, 1), stack(st_p, 2),
            stack(st_s, 1), stack(st_s, 2))
```

```python
import functools

import jax
import jax.numpy as jnp
import numpy as np
from jax import lax
from jax.experimental import pallas as pl
from jax.experimental.pallas import tpu as pltpu

F32 = jnp.float32
BF16 = jnp.bfloat16

D_MODEL = 1024
CONV_CH = D_MODEL // 2
CONV_K = 3
N_HEADS = 8
HEAD_DIM = D_MODEL // 16
ATTN_WIDTH = N_HEADS * HEAD_DIM
D_FF = 2816
ROPE_THETA = 10000.0
EPS = 1e-6
LOG2_E = 1.4426950408889634
PATTERNS = ((128, 1), (512, 4), (2048, 16))
WIN_MAX = max(w for w, _ in PATTERNS)
PAST_LEN = 8192

LANES = 128
HEADS_PER_TILE = LANES // HEAD_DIM
N_HEAD_TILES = ATTN_WIDTH // LANES
NB = 128
MAX_DIL = max(d for _, d in PATTERNS)
CHUNK_POS = NB * MAX_DIL
ATTN_INTERLEAVE = 2
FF_CHUNK = 256
ROW_TILE = 512
VMEM_LIMIT = 56 * 1024 * 1024

assert all(w // d == NB for w, d in PATTERNS)
assert [d for _, d in PATTERNS] == [1, 4, 16]


def _rms(x, g):
    return x * lax.rsqrt(jnp.mean(x * x, axis=-1, keepdims=True) + EPS) * g


def _dot(a, b):
    return jnp.dot(a, b, preferred_element_type=F32)


def _swiglu(h, w1_ref, w3_ref, w2_ref):
    gated = []
    for c in range(D_FF // FF_CHUNK):
        sl = slice(c * FF_CHUNK, (c + 1) * FF_CHUNK)
        a = _dot(h, w1_ref[:, sl])
        b = _dot(h, w3_ref[:, sl])
        gated.append((a * jax.nn.sigmoid(a) * b).astype(BF16))
    return _dot(jnp.concatenate(gated, axis=1), w2_ref[...])


def _const_spec(shape):
    return pl.BlockSpec(shape, lambda *_: (0,) * len(shape), pipeline_mode=pl.Buffered(1))


def _params(n_grid):
    return pltpu.CompilerParams(dimension_semantics=("arbitrary",) * n_grid,
                                vmem_limit_bytes=VMEM_LIMIT)


def _ffn_kernel(x_ref, g_ref, w1_ref, w3_ref, w2_ref, *rest):
    o_ref = rest[-1] if len(rest) == 1 else rest[5]
    if len(rest) > 1:
        _sample_key_pass(*rest[:5], *rest[6:])
    x = x_ref[...]
    h = _rms(x, g_ref[...]).astype(BF16)
    o_ref[...] = x + 0.5 * _swiglu(h, w1_ref, w3_ref, w2_ref)


def _ffn(x, g, w1, w3, w2, sample=None):
    m = x.shape[0]
    tm = min(ROW_TILE, m)
    n_steps = m // tm
    in_specs = [pl.BlockSpec((tm, D_MODEL), lambda i: (i, 0)),
                _const_spec((1, D_MODEL)),
                _const_spec((D_MODEL, D_FF)), _const_spec((D_MODEL, D_FF)),
                _const_spec((D_FF, D_MODEL))]
    out_shape = [jax.ShapeDtypeStruct(x.shape, F32)]
    out_specs = [pl.BlockSpec((tm, D_MODEL), lambda i: (i, 0))]
    args = [x, g, w1, w3, w2]
    if sample is not None:
        q, k_new, cache_k = sample
        n, _, _, w = cache_k.shape
        assert n == n_steps, "one cached sequence per row tile"
        bias, mult = (jnp.asarray(a) for a in _sample_masks(w))
        seq = lambda *dims: pl.BlockSpec((1,) + dims, lambda i: (i,) + (0,) * len(dims))
        in_specs += [seq(HEAD_DIM, N_HEADS), seq(HEAD_DIM, N_HEADS), _const_spec((1, w)),
                     _const_spec((1, w)), seq(N_HEADS, HEAD_DIM, w)]
        args += [q, k_new, bias, mult, cache_k]
        out_shape += [jax.ShapeDtypeStruct((n, N_HEADS, w), F32),
                      jax.ShapeDtypeStruct((n, N_HEADS, LANES), F32),
                      jax.ShapeDtypeStruct(cache_k.shape, cache_k.dtype)]
        out_specs += [seq(N_HEADS, w), seq(N_HEADS, LANES), seq(N_HEADS, HEAD_DIM, w)]
    out = pl.pallas_call(
        _ffn_kernel,
        out_shape=tuple(out_shape),
        grid=(n_steps,),
        in_specs=in_specs,
        out_specs=tuple(out_specs),
        compiler_params=_params(1),
        name="ffn1",
    )(*args)
    return out[0] if sample is None else out


def _rope_tables(pos):
    half = HEAD_DIM // 2
    inv = ROPE_THETA ** (-jnp.arange(half, dtype=F32) * 2.0 / HEAD_DIM)
    ang = pos[:, None] * inv[None, :]
    cos, sin = jnp.cos(ang), jnp.sin(ang)
    cos_t = jnp.concatenate([cos, cos] * HEADS_PER_TILE, axis=-1)
    sin_t = jnp.concatenate([-sin, sin] * HEADS_PER_TILE, axis=-1)
    return cos_t, sin_t


def _rope_tile(x, cos_t, sin_t):
    half = HEAD_DIM // 2
    lane = lax.broadcasted_iota(jnp.int32, x.shape, 1)
    first_half = (lane % HEAD_DIM) < half
    partner = jnp.where(first_half,
                        pltpu.roll(x, LANES - half, axis=1),
                        pltpu.roll(x, half, axis=1))
    return x * cos_t + partner * sin_t


def _mix_in_prompt_kernel(x_ref, g_ref, w_ref, cw_ref, gc_ref, cos_ref, sin_ref,
                          yc_ref, q_ref, k_ref, v_ref, kl_ref, vl_ref, st_ref, carry_ref,
                          *, n_keep_tiles):
    i = pl.program_id(1)
    n_i = pl.num_programs(1)
    tm = x_ref.shape[1]
    w, a = CONV_CH, ATTN_WIDTH
    h = _rms(x_ref[0], g_ref[...]).astype(BF16)

    b_g = _dot(h, w_ref[:, 0:w])
    u = _dot(h, w_ref[:, w:2 * w]) * _dot(h, w_ref[:, 2 * w:3 * w])

    @pl.when(i == 0)
    def _():
        carry_ref[...] = jnp.zeros_like(carry_ref)

    prev2 = carry_ref[0:1, :]
    prev1 = carry_ref[1:2, :]
    row = lax.broadcasted_iota(jnp.int32, (tm, w), 0)
    u1 = jnp.where(row == 0, prev1, pltpu.roll(u, 1, axis=0))
    u2 = jnp.where(row == 0, prev2, jnp.where(row == 1, prev1, pltpu.roll(u, 2, axis=0)))
    conv = cw_ref[0:1, :] * u2 + cw_ref[1:2, :] * u1 + cw_ref[2:3, :] * u
    yc_ref[0] = _rms(b_g * conv, gc_ref[...])
    carry_ref[0:2, :] = u[tm - 2:tm, :]
    st_ref[0] = u[tm - 2:tm, :]

    o = 3 * w
    q = _dot(h, w_ref[:, o:o + a])
    k = _dot(h, w_ref[:, o + a:o + 2 * a])
    v = _dot(h, w_ref[:, o + 2 * a:o + 3 * a])
    cos_t, sin_t = cos_ref[...], sin_ref[...]
    k_tiles = []
    for t in range(N_HEAD_TILES):
        sl = slice(t * LANES, (t + 1) * LANES)
        q_ref[0, t] = _rope_tile(q[:, sl], cos_t, sin_t) * (HEAD_DIM ** -0.5 * LOG2_E)
        kt = _rope_tile(k[:, sl], cos_t, sin_t)
        k_ref[0, t] = kt
        v_ref[0, t] = v[:, sl]
        k_tiles.append(kt)

    @pl.when(i >= n_i - n_keep_tiles)
    def _():
        kl_ref[0] = jnp.concatenate(k_tiles, axis=1).T
        vl_ref[0] = v.T


def _mix_in_prompt(x, g, w_in, conv_w, g_conv, cos_t, sin_t):
    b, s, _ = x.shape
    tm = min(ROW_TILE, s)
    n_i = s // tm
    keep = min(WIN_MAX, s)
    n_keep = keep // tm
    tile_map = lambda bi, i: (bi, 0, i, 0)
    keep_map = lambda bi, i: (bi, 0, jnp.maximum(i - (n_i - n_keep), 0))
    head_major = jax.ShapeDtypeStruct((b, N_HEAD_TILES, s, LANES), F32)
    return pl.pallas_call(
        functools.partial(_mix_in_prompt_kernel, n_keep_tiles=n_keep),
        out_shape=(jax.ShapeDtypeStruct((b, s, CONV_CH), F32),
                   head_major, head_major, head_major,
                   jax.ShapeDtypeStruct((b, ATTN_WIDTH, keep), F32),
                   jax.ShapeDtypeStruct((b, ATTN_WIDTH, keep), F32),
                   jax.ShapeDtypeStruct((b, CONV_K - 1, CONV_CH), F32)),
        grid=(b, n_i),
        in_specs=[pl.BlockSpec((1, tm, D_MODEL), lambda bi, i: (bi, i, 0)),
                  _const_spec((1, D_MODEL)),
                  _const_spec(w_in.shape),
                  _const_spec((CONV_K, CONV_CH)),
                  _const_spec((1, CONV_CH)),
                  pl.BlockSpec((tm, LANES), lambda bi, i: (i, 0)),
                  pl.BlockSpec((tm, LANES), lambda bi, i: (i, 0))],
        out_specs=(pl.BlockSpec((1, tm, CONV_CH), lambda bi, i: (bi, i, 0)),
                   pl.BlockSpec((1, N_HEAD_TILES, tm, LANES), tile_map),
                   pl.BlockSpec((1, N_HEAD_TILES, tm, LANES), tile_map),
                   pl.BlockSpec((1, N_HEAD_TILES, tm, LANES), tile_map),
                   pl.BlockSpec((1, ATTN_WIDTH, tm), keep_map),
                   pl.BlockSpec((1, ATTN_WIDTH, tm), keep_map),
                   pl.BlockSpec((1, CONV_K - 1, CONV_CH), lambda bi, i: (bi, 0, 0))),
        scratch_shapes=[pltpu.VMEM((8, CONV_CH), F32)],
        compiler_params=_params(2),
        name="mix_in_prompt",
    )(x, g, w_in, conv_w, g_conv, cos_t, sin_t)


def _band_bias():
    dist = np.arange(NB)[:, None] - np.arange(2 * NB)[None, :]
    out = [np.where((dist + off >= 0) & (dist + off <= NB), 0.0, -np.inf) for off in (NB, 0)]
    return np.stack(out).astype(np.float32)


def _attn_scores(q, k, bias):
    lo = lax.broadcasted_iota(jnp.int32, (NB, LANES), 1) < HEAD_DIM
    kb = k.astype(BF16)
    nt = (((1,), (1,)), ((), ()))
    return [lax.dot_general(qh.astype(BF16), kb, nt, preferred_element_type=F32) + bias
            for qh in (jnp.where(lo, q, 0.0), jnp.where(lo, 0.0, q))]


def _attn_update(scores, v, old):
    lo = lax.broadcasted_iota(jnp.int32, (NB, LANES), 1) < HEAD_DIM
    vb = v.astype(BF16)
    m_new, row_sum, pv = [], [], []
    for hi, s in enumerate(scores):
        row_max = jnp.max(s, axis=-1, keepdims=True)
        if old is None:
            mh = jnp.broadcast_to(row_max, (NB, LANES))
        else:
            mh = jnp.maximum(old[1 + hi], row_max)
        p = jnp.exp2(s - jnp.concatenate([mh, mh], axis=1))
        row_sum.append(jnp.sum(p, axis=-1, keepdims=True))
        pv.append(_dot(p.astype(BF16), vb))
        m_new.append(mh)
    pv = jnp.where(lo, pv[0], pv[1])
    rs = jnp.where(lo, row_sum[0], row_sum[1])
    if old is None:
        return pv, m_new[0], m_new[1], rs
    alpha = jnp.where(lo, jnp.exp2(old[1] - m_new[0]), jnp.exp2(old[2] - m_new[1]))
    return alpha * old[0] + pv, m_new[0], m_new[1], alpha * old[3] + rs


def _attn_groups():
    groups = []
    for d, with_old, final in ((16, False, False), (4, True, False), (1, True, True)):
        per_seg = min(d, ATTN_INTERLEAVE)
        segs_per_group = ATTN_INTERLEAVE // per_seg
        n_seg = CHUNK_POS // (NB * d)
        for r0 in range(0, d, per_seg):
            for seg0 in range(0, n_seg, segs_per_group):
                starts = [(seg0 + sg) * NB * d + r0 + r
                          for sg in range(segs_per_group) for r in range(per_seg)]
                groups.append((d, starts, with_old, final))
    return groups


def _attn_prompt_kernel(bias_ref, q_ref, k_ref, v_ref, o_ref, acc_ref, m0_ref, m1_ref, l_ref, sc_ref):
    pos0 = pl.program_id(2) * CHUNK_POS
    state_refs = (acc_ref, m0_ref, m1_ref, l_ref)

    def rows(start, n, d):
        return pl.ds(start, n) if d == 1 else pl.ds(start, n, stride=d)

    def key_window(d, q0):
        start = pos0 + q0 - NB * d
        first = start < 0
        ks = jnp.where(first, start + NB * d, start)
        if d == 1:
            ks = pl.multiple_of(ks, NB)
        return ks, first.astype(jnp.int32)

    def issue_scores(slot, group):
        d, starts, _, _ = group
        for bi, q0 in enumerate(starts):
            ks, variant = key_window(d, q0)
            scores = _attn_scores(q_ref[0, 0, rows(q0, NB, d), :],
                                  k_ref[0, 0, rows(ks, 2 * NB, d), :], bias_ref[variant])
            for hi, s in enumerate(scores):
                sc_ref[slot, bi, hi] = s

    def fold(slot, group):
        d, starts, with_old, final = group
        for bi, q0 in enumerate(starts):
            ks, _ = key_window(d, q0)
            old = tuple(ref[rows(q0, NB, d), :] for ref in state_refs) if with_old else None
            new = _attn_update([sc_ref[slot, bi, hi] for hi in range(HEADS_PER_TILE)],
                               v_ref[0, 0, rows(ks, 2 * NB, d), :], old)
            if final:
                o_ref[0, 0, rows(q0, NB, d), :] = new[0] / new[3]
            else:
                for ref, val in zip(state_refs, new):
                    ref[rows(q0, NB, d), :] = val

    groups = _attn_groups()
    issue_scores(0, groups[0])
    for g, group in enumerate(groups):
        if g + 1 < len(groups):
            issue_scores((g + 1) % 2, groups[g + 1])
        fold(g % 2, group)


def _attn_prompt(q, k, v):
    b, _, s, _ = q.shape
    assert s % CHUNK_POS == 0
    bias = jnp.asarray(_band_bias())
    chunk_spec = pl.BlockSpec((1, 1, CHUNK_POS, LANES), lambda bi, t, c: (bi, t, c, 0))
    seq_spec = pl.BlockSpec((1, 1, s, LANES), lambda bi, t, c: (bi, t, 0, 0))
    return pl.pallas_call(
        _attn_prompt_kernel,
        out_shape=jax.ShapeDtypeStruct(q.shape, F32),
        grid=(b, N_HEAD_TILES, s // CHUNK_POS),
        in_specs=[_const_spec(bias.shape), chunk_spec, seq_spec, seq_spec],
        out_specs=chunk_spec,
        scratch_shapes=[pltpu.VMEM((CHUNK_POS, LANES), F32)] * 4
        + [pltpu.VMEM((2, ATTN_INTERLEAVE, HEADS_PER_TILE, NB, 2 * NB), F32)],
        compiler_params=_params(3),
        name="attn_prompt",
    )(bias, q, k, v)


def _mix_out_kernel(x_ref, yc_ref, at_ref, ga_ref, wo_ref, g_ref, w1_ref, w3_ref, w2_ref,
                    gf_ref, *rest):
    o_ref = rest[-1] if len(rest) == 1 else rest[4]
    if len(rest) > 1:
        _sample_value_pass(*rest[:4], *rest[5:])
    attn = jnp.concatenate([at_ref[0, t] for t in range(N_HEAD_TILES)], axis=1)
    y_a = _rms(attn, ga_ref[...])
    mixed = jnp.concatenate([yc_ref[0], y_a], axis=1).astype(BF16)
    x = x_ref[0] + _dot(mixed, wo_ref[...])
    h = _rms(x, g_ref[...]).astype(BF16)
    x = x + 0.5 * _swiglu(h, w1_ref, w3_ref, w2_ref)
    o_ref[0] = _rms(x, gf_ref[...])


def _mix_out(x, yc, attn, g_attn, w_out, g, w1, w3, w2, g_final, sample=None):
    b, s, _ = x.shape
    tm = min(ROW_TILE, s)
    n_i = s // tm
    row_map = lambda bi, i: (bi, i, 0)
    in_specs = [pl.BlockSpec((1, tm, D_MODEL), row_map),
                pl.BlockSpec((1, tm, CONV_CH), row_map),
                pl.BlockSpec((1, N_HEAD_TILES, tm, LANES), lambda bi, i: (bi, 0, i, 0)),
                _const_spec((1, ATTN_WIDTH)),
                _const_spec(w_out.shape),
                _const_spec((1, D_MODEL)),
                _const_spec((D_MODEL, D_FF)), _const_spec((D_MODEL, D_FF)),
                _const_spec((D_FF, D_MODEL)),
                _const_spec((1, D_MODEL))]
    out_shape = [jax.ShapeDtypeStruct(x.shape, F32)]
    out_specs = [pl.BlockSpec((1, tm, D_MODEL), row_map)]
    args = [x, yc, attn, g_attn, w_out, g, w1, w3, w2, g_final]
    if sample is not None:
        weights, self_weight, v_new, cache_v = sample
        n, _, _, w = cache_v.shape
        assert n == b * n_i, "one cached sequence per row tile"
        seq = lambda *dims: pl.BlockSpec((1,) + dims,
                                         lambda bi, i: (bi * n_i + i,) + (0,) * len(dims))
        in_specs += [seq(N_HEADS, w), seq(N_HEADS, LANES), seq(HEAD_DIM, N_HEADS),
                     seq(N_HEADS, HEAD_DIM, w)]
        args += [weights, self_weight, v_new, cache_v]
        out_shape += [jax.ShapeDtypeStruct((n, HEAD_DIM, LANES), F32),
                      jax.ShapeDtypeStruct(cache_v.shape, cache_v.dtype)]
        out_specs += [seq(HEAD_DIM, LANES), seq(N_HEADS, HEAD_DIM, w)]
    out = pl.pallas_call(
        _mix_out_kernel,
        out_shape=tuple(out_shape),
        grid=(b, n_i),
        in_specs=in_specs,
        out_specs=tuple(out_specs),
        compiler_params=_params(2),
        name="mix_out_ffn2",
    )(*args)
    return out[0] if sample is None else out


def _mix_in_sample_kernel(x_ref, g_ref, w_ref, cw_ref, gc_ref, cos_ref, sin_ref, s0_ref, s1_ref,
                          yc_ref, q_ref, k_ref, v_ref, u_ref):
    w, a = CONV_CH, ATTN_WIDTH
    h = _rms(x_ref[...], g_ref[...]).astype(BF16)
    b_g = _dot(h, w_ref[:, 0:w])
    u = _dot(h, w_ref[:, w:2 * w]) * _dot(h, w_ref[:, 2 * w:3 * w])
    conv = cw_ref[0:1, :] * s0_ref[...] + cw_ref[1:2, :] * s1_ref[...] + cw_ref[2:3, :] * u
    yc_ref[...] = _rms(b_g * conv, gc_ref[...])
    u_ref[...] = u
    o = 3 * w
    q = _dot(h, w_ref[:, o:o + a])
    k = _dot(h, w_ref[:, o + a:o + 2 * a])
    v_ref[...] = _dot(h, w_ref[:, o + 2 * a:o + 3 * a])
    cos_t, sin_t = cos_ref[...], sin_ref[...]
    for t in range(N_HEAD_TILES):
        sl = slice(t * LANES, (t + 1) * LANES)
        q_ref[:, sl] = _rope_tile(q[:, sl], cos_t, sin_t) * (HEAD_DIM ** -0.5)
        k_ref[:, sl] = _rope_tile(k[:, sl], cos_t, sin_t)


def _mix_in_sample(x, g, w_in, conv_w, g_conv, cos_t, sin_t, st0, st1):
    n = x.shape[0]
    vec = lambda width: jax.ShapeDtypeStruct((n, width), F32)
    return pl.pallas_call(
        _mix_in_sample_kernel,
        out_shape=(vec(CONV_CH), vec(ATTN_WIDTH), vec(ATTN_WIDTH), vec(ATTN_WIDTH), vec(CONV_CH)),
        compiler_params=pltpu.CompilerParams(vmem_limit_bytes=VMEM_LIMIT),
        name="mix_in_sample",
    )(x, g, w_in, conv_w, g_conv, cos_t, sin_t, st0, st1)


def _sample_masks(w):
    j = w - np.arange(w)
    mult = sum(((j % d == 0) & (j <= win)).astype(np.float32) for win, d in PATTERNS)
    bias = np.where(mult > 0, 0.0, -np.inf).astype(np.float32)
    return bias.reshape(1, w), mult.reshape(1, w)


def _lane_tile_sum(x):
    part = x[:, 0:LANES]
    for t in range(1, x.shape[1] // LANES):
        part = part + x[:, t * LANES:(t + 1) * LANES]
    return part


def _shifted(cache, new_col):
    w = cache.shape[-1]
    newest = lax.broadcasted_iota(jnp.int32, cache.shape, 1) == w - 1
    return jnp.where(newest, new_col, pltpu.roll(cache, w - 1, axis=1))


def _sample_key_pass(q_ref, kn_ref, bias_ref, mult_ref, ck_ref, p_ref, ps_ref, ok_ref):
    n_pat = float(len(PATTERNS))
    qt, knt = q_ref[0], kn_ref[0]
    s_self_all = jnp.sum(qt * knt, axis=0, keepdims=True)
    for h in range(N_HEADS):
        col = slice(h, h + 1)
        kh = ck_ref[0, h]
        s = jnp.sum(kh * qt[:, col], axis=0, keepdims=True) + bias_ref[...]
        s_self = s_self_all[:, col]
        m = jnp.maximum(jnp.max(s, axis=-1, keepdims=True), s_self)
        e = jnp.exp(s - m) * mult_ref[...]
        e_self = n_pat * jnp.exp(s_self - m)
        den = jnp.sum(_lane_tile_sum(e), axis=-1, keepdims=True) + e_self
        p_ref[0, col, :] = e / den
        ps_ref[0, col, :] = jnp.broadcast_to(e_self / den, (1, LANES))
        ok_ref[0, h] = _shifted(kh, knt[:, col])


def _sample_value_pass(p_ref, ps_ref, vn_ref, cv_ref, o_ref, ov_ref):
    vnt = vn_ref[0]
    head_lane = lax.broadcasted_iota(jnp.int32, (HEAD_DIM, LANES), 1)
    out = jnp.zeros((HEAD_DIM, LANES), F32)
    for h in range(N_HEADS):
        col = slice(h, h + 1)
        vh = cv_ref[0, h]
        num = jnp.sum(_lane_tile_sum(vh * p_ref[0, col, :]), axis=-1, keepdims=True)
        num = num + ps_ref[0, col, 0:1] * vnt[:, col]
        out = jnp.where(head_lane == h, num, out)
        ov_ref[0, h] = _shifted(vh, vnt[:, col])
    o_ref[0] = out


def _layer(xp, xs, conv_state, cache_k, cache_v, wts):
    b, s, _ = xp.shape
    n = xs.shape[0]
    assert xs.shape[1] == 1, "one new token per cached sequence"
    assert cache_k.shape[1] == WIN_MAX, "every window must lie inside the cached range"
    mix = (wts["g_mix"], wts["w_in"], wts["conv_w"], wts["g_conv"])
    tail = (wts["g_attn"], wts["w_out"], *wts["ffn2"], wts["g_final"])

    x1s = _ffn(xs.reshape(n, D_MODEL), *wts["ffn1"])
    st0, st1 = conv_state[:, 0], conv_state[:, 1]
    ycs, qs, ks, vs, us = _mix_in_sample(x1s, *mix, *_rope_tables(jnp.full((1,), PAST_LEN, F32)),
                                         st0, st1)
    dh_head = lambda t: t.reshape(n, N_HEADS, HEAD_DIM).transpose(0, 2, 1)
    pos_minor = lambda c: c.transpose(0, 2, 3, 1)
    pos_major = lambda c: c.transpose(0, 3, 1, 2)

    x1, weights, self_weight, new_k = _ffn(xp.reshape(b * s, D_MODEL), *wts["ffn1"],
                                           sample=(dh_head(qs), dh_head(ks), pos_minor(cache_k)))
    x1 = x1.reshape(b, s, D_MODEL)
    yc, q, k, v, kt_last, vt_last, u_state = _mix_in_prompt(
        x1, *mix, *_rope_tables(jnp.arange(s, dtype=F32)))
    attn = _attn_prompt(q, k, v)
    yp, attn_s, new_v = _mix_out(x1, yc, attn, *tail,
                                 sample=(weights, self_weight, dh_head(vs), pos_minor(cache_v)))
    to_cache = lambda t: pos_major(t.reshape(b, N_HEADS, HEAD_DIM, -1))
    state_p = (u_state, to_cache(kt_last), to_cache(vt_last))

    attn_s = attn_s[:, :, :N_HEADS].transpose(0, 2, 1).reshape(n, N_HEAD_TILES, LANES)
    attn_s = attn_s.transpose(1, 0, 2)[None]
    ys = _mix_out(x1s[None], ycs[None], attn_s, *tail).reshape(n, 1, D_MODEL)
    state_s = (jnp.stack([st1, us], axis=1), pos_major(new_k), pos_major(new_v))
    return yp, ys, state_p, state_s


def kernel(x_prompt, x_sample, state_conv, cache_k, cache_v, norm_ffn1, ffn1_w1, ffn1_w3, ffn1_w2,
           norm_mix, w_in, conv_w, out_norm_conv, out_norm_attn, w_out, norm_ffn2, ffn2_w1,
           ffn2_w3, ffn2_w2, norm_final):
    depth = w_in.shape[0]
    assert depth == 1, "the final norm is fused into the (single) layer's last kernel"
    row = lambda g: g.reshape(1, -1)
    lo = lambda w: w.astype(BF16)
    st_p, st_s = [], []
    yp, ys = x_prompt, x_sample
    for l in range(depth):
        wts = dict(ffn1=(row(norm_ffn1[l]), lo(ffn1_w1[l]), lo(ffn1_w3[l]), lo(ffn1_w2[l])),
                   ffn2=(row(norm_ffn2[l]), lo(ffn2_w1[l]), lo(ffn2_w3[l]), lo(ffn2_w2[l])),
                   g_mix=row(norm_mix[l]), w_in=lo(w_in[l]), conv_w=conv_w[l],
                   g_conv=row(out_norm_conv[l]), g_attn=row(out_norm_attn[l]),
                   w_out=lo(w_out[l]), g_final=row(norm_final))
        yp, ys, sp, ss = _layer(yp, ys, state_conv[l], cache_k[l], cache_v[l], wts)
        st_p.append(sp)
        st_s.append(ss)
    stack = lambda states, j: jnp.stack([s[j] for s in states], axis=0)
    return (yp, ys, stack(st_p, 0), stack(st_s, 0), stack(st_p, 1), stack(st_p, 2),
            stack(st_s, 1), stack(st_s, 2))
```
